```python
import jax, jax.numpy as jnp
from jax import lax
import numpy as np

D_MODEL = 2048
BATCH = 4
SEQ = 4096
DEPTH = 1

D_FF = 5632
FFN_RES_SCALE = 0.5
FOURIER_WIDTH = D_MODEL // 2
FOURIER_GROUPS = 4
FOURIER_GROUP_DIM = FOURIER_WIDTH // FOURIER_GROUPS
LRU_WIDTH = D_MODEL // 2
LRU_HEADS = 8
LRU_HEAD_DIM = LRU_WIDTH // LRU_HEADS
LRU_C = 8.0
N_DIRECTIONS = 2
CONV_WIDTH = 4
CONV_PAD_LEFT = 2
CONV_PAD_RIGHT = CONV_WIDTH - 1 - CONV_PAD_LEFT
N_BRANCHES = 2
IN_WIDTH = FOURIER_WIDTH + 2 * LRU_WIDTH + N_BRANCHES * D_MODEL
RMS_EPS = 1e-6

kernel_name = 'hybrid_fnet_rglru_macaron_encoder'


def rms_norm(x, g):
    xf = x.astype(jnp.float32)
    y = xf * lax.rsqrt(jnp.mean(xf * xf, axis=-1, keepdims=True) + RMS_EPS)
    return (y * g.astype(jnp.float32)).astype(x.dtype)


def swiglu(h, w_gate, w_up, w_down):
    return (jax.nn.silu(h @ w_gate) * (h @ w_up)) @ w_down


def fourier_mix(z):
    b, s, _ = z.shape
    zg = z.astype(jnp.float32).reshape(b, s, FOURIER_GROUPS, FOURIER_GROUP_DIM)
    y = jnp.fft.fft2(zg, axes=(1, 3), norm='ortho').real
    return y.reshape(b, s, FOURIER_WIDTH).astype(z.dtype)


def centred_dwconv(v, w, bias):
    s = v.shape[1]
    vp = jnp.pad(v, ((0, 0), (CONV_PAD_LEFT, CONV_PAD_RIGHT), (0, 0)))
    out = vp[:, 0:s] * w[0]
    for k in range(1, CONV_WIDTH):
        out = out + vp[:, k:k + s] * w[k]
    return out + bias


def _combine(left, right):
    a_l, b_l = left
    a_r, b_r = right
    return a_l * a_r, a_r * b_l + b_r


def linear_scan(a, u, reverse):
    _, h = lax.associative_scan(_combine, (a, u), axis=1, reverse=reverse)
    return h


def rglru_bidir(v, wa, ba, wx, bx, lam):
    b, s, _ = v.shape
    vh = v.reshape(b, s, LRU_HEADS, LRU_HEAD_DIM)
    r = jax.nn.sigmoid((jnp.einsum('bshi,dhij->dbshj', vh, wa) + ba[:, None, None]).astype(jnp.float32))
    i = jax.nn.sigmoid((jnp.einsum('bshi,dhij->dbshj', vh, wx) + bx[:, None, None]).astype(jnp.float32))
    log_a = -LRU_C * r * jax.nn.softplus(-lam.astype(jnp.float32))[:, None, None]
    a = jnp.exp(log_a)
    u = jnp.sqrt(-jnp.expm1(2.0 * log_a)) * i * vh.astype(jnp.float32)[None]
    h = linear_scan(a[0], u[0], reverse=False) + linear_scan(a[1], u[1], reverse=True)
    return h.reshape(b, s, LRU_WIDTH).astype(v.dtype)


def setup_inputs(seed: int = 0) -> dict:
    key = jax.random.key(seed)
    ks = jax.random.split(key, 24)
    L, D, F = DEPTH, D_MODEL, D_FF
    H, Hd = LRU_HEADS, LRU_HEAD_DIM

    def nrm(k, shape, fan_in):
        return jax.random.normal(k, shape, jnp.float32) * (fan_in ** -0.5)

    def gain(k, shape):
        return 1.0 + 0.01 * jax.random.normal(k, shape, jnp.float32)

    def bias(k, shape):
        return 0.01 * jax.random.normal(k, shape, jnp.float32)

    a_c = jax.random.uniform(ks[14], (L, N_DIRECTIONS, H, Hd), jnp.float32, 0.9, 0.999)
    s_lam = a_c ** (1.0 / LRU_C)
    lru_lambda = jnp.log(s_lam) - jnp.log1p(-s_lam)

    return {
        'x': jax.random.normal(ks[0], (BATCH, SEQ, D), jnp.float32),
        'ffn1_norm': gain(ks[1], (L, D)),
        'ffn1_w_gate': nrm(ks[2], (L, D, F), D),
        'ffn1_w_up': nrm(ks[3], (L, D, F), D),
        'ffn1_w_down': nrm(ks[4], (L, F, D), F),
        'mix_norm': gain(ks[5], (L, D)),
        'w_in': nrm(ks[6], (L, D, IN_WIDTH), D),
        'b_gates': bias(ks[7], (L, N_BRANCHES, D)),
        'conv_w': nrm(ks[8], (L, CONV_WIDTH, LRU_WIDTH), CONV_WIDTH),
        'conv_b': bias(ks[9], (L, LRU_WIDTH)),
        'lru_wa': nrm(ks[10], (L, N_DIRECTIONS, H, Hd, Hd), Hd),
        'lru_ba': bias(ks[11], (L, N_DIRECTIONS, H, Hd)),
        'lru_wx': nrm(ks[12], (L, N_DIRECTIONS, H, Hd, Hd), Hd),
        'lru_bx': bias(ks[13], (L, N_DIRECTIONS, H, Hd)),
        'lru_lambda': lru_lambda,
        'proj_a': nrm(ks[15], (L, FOURIER_WIDTH, D), FOURIER_WIDTH),
        'proj_b': nrm(ks[16], (L, LRU_WIDTH, D), LRU_WIDTH),
        'w_out': nrm(ks[17], (L, D, D), D),
        'ffn2_norm': gain(ks[18], (L, D)),
        'ffn2_w_gate': nrm(ks[19], (L, D, F), D),
        'ffn2_w_up': nrm(ks[20], (L, D, F), D),
        'ffn2_w_down': nrm(ks[21], (L, F, D), F),
        'final_norm': gain(ks[22], (D,)),
    }


def reference(x, ffn1_norm, ffn1_w_gate, ffn1_w_up, ffn1_w_down, mix_norm, w_in, b_gates,
              conv_w, conv_b, lru_wa, lru_ba, lru_wx, lru_bx, lru_lambda, proj_a, proj_b, w_out,
              ffn2_norm, ffn2_w_gate, ffn2_w_up, ffn2_w_down, final_norm):
    b, s, _ = x.shape
    split_points = [FOURIER_WIDTH, FOURIER_WIDTH + LRU_WIDTH, FOURIER_WIDTH + 2 * LRU_WIDTH]
    for l in range(DEPTH):
        h = rms_norm(x, ffn1_norm[l])
        x = x + FFN_RES_SCALE * swiglu(h, ffn1_w_gate[l], ffn1_w_up[l], ffn1_w_down[l])

        u = rms_norm(x, mix_norm[l])
        z = u @ w_in[l]
        z_four, z_rec, z_gelu, z_gates = jnp.split(z, split_points, axis=-1)
        gates = jax.nn.sigmoid(z_gates.reshape(b, s, N_BRANCHES, D_MODEL) + b_gates[l])

        y_a = fourier_mix(z_four) @ proj_a[l]

        v = centred_dwconv(z_rec, conv_w[l], conv_b[l])
        y_rec = rglru_bidir(v, lru_wa[l], lru_ba[l], lru_wx[l], lru_bx[l], lru_lambda[l])
        y_b = (y_rec * jax.nn.gelu(z_gelu)) @ proj_b[l]

        merged = gates[:, :, 0, :] * y_a + gates[:, :, 1, :] * y_b
        x = x + merged @ w_out[l]

        h = rms_norm(x, ffn2_norm[l])
        x = x + FFN_RES_SCALE * swiglu(h, ffn2_w_gate[l], ffn2_w_up[l], ffn2_w_down[l])
    return rms_norm(x, final_norm)
```

```python
import functools
import math

import numpy as np
import jax
import jax.numpy as jnp
from jax import lax
from jax.experimental import pallas as pl
from jax.experimental.pallas import tpu as pltpu

D_MODEL = 2048
D_FF = 5632
FOURIER_WIDTH = 1024
FOURIER_GROUPS = 4
GROUP_DIM = FOURIER_WIDTH // FOURIER_GROUPS
LRU_WIDTH = 1024
LRU_HEADS = 8
HEAD_DIM = LRU_WIDTH // LRU_HEADS
LRU_C = 8.0
RMS_EPS = 1e-6
FFN_RES_SCALE = 0.5
GATE_WIDTH = 2 * D_MODEL
IN_WIDTH = FOURIER_WIDTH + 2 * LRU_WIDTH + GATE_WIDTH

VMEM_LIMIT_BYTES = 56 * 1024 * 1024
SUBLANES = 8
LANES = 128

FFT_BLOCKS = 16
SCAN_CHUNK = 128
SCAN_PITCH = SCAN_CHUNK + SUBLANES

_BF16 = jnp.bfloat16
_F32 = jnp.float32


def _dot(a, b):
    return jnp.dot(a, b, preferred_element_type=_F32)


def _sigmoid(x):
    return 0.5 * (1.0 + jnp.tanh(0.5 * x))


def _rms_norm(x, gain):
    ms = jnp.mean(x * x, axis=-1, keepdims=True)
    return x * lax.rsqrt(ms + RMS_EPS) * gain


def _gelu_tanh(x):
    return 0.5 * x * (1.0 + jnp.tanh(math.sqrt(2.0 / math.pi) * (x + 0.044715 * (x * x * x))))


def _ffn_kernel(x_ref, gain_ref, wg_ref, wu_ref, wd_ref, post_ref, *refs, emit_residual):
    if emit_residual:
        res_ref, normed_ref, h_ref, acc_ref = refs
    else:
        normed_ref, h_ref, acc_ref = refs
    j = pl.program_id(1)

    @pl.when(j == 0)
    def _():
        h_ref[...] = _rms_norm(x_ref[...], gain_ref[...]).astype(_BF16)
        acc_ref[...] = jnp.zeros_like(acc_ref)

    h = h_ref[...]
    g = _dot(h, wg_ref[...])
    u = _dot(h, wu_ref[...])
    act = (g * _sigmoid(g) * u).astype(_BF16)
    acc_ref[...] += _dot(act, wd_ref[...])

    @pl.when(j == pl.num_programs(1) - 1)
    def _():
        y = x_ref[...] + FFN_RES_SCALE * acc_ref[...]
        if emit_residual:
            res_ref[...] = y
        normed_ref[...] = _rms_norm(y, post_ref[...]).astype(normed_ref.dtype)


def _ffn_call(x, gain, wg, wu, wd, post_gain, *, emit_residual, tm=512, tf=512):
    t, d = x.shape
    f = wg.shape[1]
    grid = (t // tm, f // tf)
    row_spec = pl.BlockSpec((tm, d), lambda i, j: (i, 0))
    vec_spec = pl.BlockSpec((1, d), lambda i, j: (0, 0))
    in_specs = [
        row_spec,
        vec_spec,
        pl.BlockSpec((d, tf), lambda i, j: (0, j)),
        pl.BlockSpec((d, tf), lambda i, j: (0, j)),
        pl.BlockSpec((tf, d), lambda i, j: (j, 0)),
        vec_spec,
    ]
    if emit_residual:
        out_shape = (jax.ShapeDtypeStruct((t, d), _F32), jax.ShapeDtypeStruct((t, d), _BF16))
        out_specs = (row_spec, row_spec)
    else:
        out_shape = jax.ShapeDtypeStruct((t, d), _F32)
        out_specs = row_spec
    return pl.pallas_call(
        functools.partial(_ffn_kernel, emit_residual=emit_residual),
        grid=grid,
        in_specs=in_specs,
        out_specs=out_specs,
        out_shape=out_shape,
        scratch_shapes=[pltpu.VMEM((tm, d), _BF16), pltpu.VMEM((tm, d), _F32)],
        compiler_params=pltpu.CompilerParams(
            dimension_semantics=("parallel", "arbitrary"), vmem_limit_bytes=VMEM_LIMIT_BYTES),
        name="ffn_residual" if emit_residual else "ffn_final",
    )(x, gain, wg, wu, wd, post_gain)


def _inproj_kernel(u_ref, w_ref, bg_ref, zf_ref, zr_ref, zg_ref, gates_ref):
    j = pl.program_id(1)
    z = _dot(u_ref[...], w_ref[...])

    @pl.when(j == 0)
    def _():
        zf_ref[...] = z.astype(_BF16)

    @pl.when(j == 1)
    def _():
        zr_ref[...] = z.astype(_BF16)

    @pl.when(j == 2)
    def _():
        zg_ref[...] = z.astype(_BF16)

    @pl.when(j >= 3)
    def _():
        gates_ref[...] = _sigmoid(z + bg_ref[...]).astype(_BF16)


def _inproj_call(u, w_in, b_gates, *, tm=512, tn=1024):
    t, d = u.shape
    n_side = (FOURIER_WIDTH + 2 * LRU_WIDTH) // tn
    grid = (t // tm, IN_WIDTH // tn)
    side_spec = pl.BlockSpec((tm, tn), lambda i, j: (i, 0))
    gate_col = lambda i, j: jnp.maximum(j - n_side, 0)
    return pl.pallas_call(
        _inproj_kernel,
        grid=grid,
        in_specs=[
            pl.BlockSpec((tm, d), lambda i, j: (i, 0)),
            pl.BlockSpec((d, tn), lambda i, j: (0, j)),
            pl.BlockSpec((1, tn), lambda i, j: (0, gate_col(i, j))),
        ],
        out_specs=(side_spec, side_spec, side_spec,
                   pl.BlockSpec((tm, tn), lambda i, j: (i, gate_col(i, j)))),
        out_shape=(
            jax.ShapeDtypeStruct((t, FOURIER_WIDTH), _BF16),
            jax.ShapeDtypeStruct((t, LRU_WIDTH), _BF16),
            jax.ShapeDtypeStruct((t, LRU_WIDTH), _BF16),
            jax.ShapeDtypeStruct((t, GATE_WIDTH), _BF16),
        ),
        compiler_params=pltpu.CompilerParams(
            dimension_semantics=("parallel", "arbitrary"), vmem_limit_bytes=VMEM_LIMIT_BYTES),
        name="inproj",
    )(u, w_in, b_gates)


def _fourier_constants(seq):
    rows = seq // FFT_BLOCKS
    c = np.arange(GROUP_DIM)
    ang_c = 2.0 * np.pi * np.outer(c, c) / GROUP_DIM
    chan = np.concatenate([np.cos(ang_c), -np.sin(ang_c)], axis=1) / math.sqrt(GROUP_DIM)
    n1 = np.arange(rows)
    k1 = np.arange(rows)
    mats = []
    for k2 in range(FFT_BLOCKS):
        ang = 2.0 * np.pi * np.outer(FFT_BLOCKS * k1 + k2, n1) / seq
        mats.append(np.concatenate([np.cos(ang), np.sin(ang)], axis=1) / math.sqrt(seq))
    seq_mats = np.stack(mats)
    tile = FFT_BLOCKS * FFT_BLOCKS
    rho = np.arange(tile)
    perm = np.zeros((tile, tile), np.float32)
    perm[rho, FFT_BLOCKS * (rho % FFT_BLOCKS) + rho // FFT_BLOCKS] = 1.0
    return (jnp.asarray(chan, _F32), jnp.asarray(seq_mats, _F32), jnp.asarray(perm, _F32))


def _fft_across_blocks(xs):
    n = len(xs)
    if n == 1:
        return xs
    ev = _fft_across_blocks(xs[0::2])
    od = _fft_across_blocks(xs[1::2])
    out = [None] * n
    for k in range(n // 2):
        o_r, o_i = od[k]
        e_r, e_i = ev[k]
        if k == 0:
            t_r, t_i = o_r, o_i
        elif 4 * k == n:
            out[k] = (e_r + o_i, e_i - o_r)
            out[k + n // 2] = (e_r - o_i, e_i + o_r)
            continue
        else:
            c = math.cos(2.0 * math.pi * k / n)
            s = -math.sin(2.0 * math.pi * k / n)
            t_r = c * o_r - s * o_i
            t_i = c * o_i + s * o_r
        out[k] = (e_r + t_r, e_i + t_i)
        out[k + n // 2] = (e_r - t_r, e_i - t_i)
    return out


def _fourier_kernel(z_ref, chan_ref, seq_ref, perm_ref, o_ref, w_ref, b_ref, g_ref):
    seq = z_ref.shape[1]
    rows = seq // FFT_BLOCKS
    gd = GROUP_DIM

    for blk in range(FFT_BLOCKS):
        sl = pl.ds(blk * rows, rows)
        w_ref[sl, :] = _dot(z_ref[0, sl, :], chan_ref[...])

    def butterfly(r, carry):
        r0 = pl.multiple_of(r * SUBLANES, SUBLANES)
        for lc in range(gd // LANES):
            re_l = pl.ds(lc * LANES, LANES)
            im_l = pl.ds(gd + lc * LANES, LANES)
            xs = [(w_ref[pl.ds(blk * rows + r0, SUBLANES), re_l],
                   w_ref[pl.ds(blk * rows + r0, SUBLANES), im_l]) for blk in range(FFT_BLOCKS)]
            ys = _fft_across_blocks(xs)
            for k2 in range(FFT_BLOCKS):
                b_ref[k2, pl.ds(r0, SUBLANES), pl.ds(lc * LANES, LANES)] = ys[k2][0]
                b_ref[k2, pl.ds(rows + r0, SUBLANES), pl.ds(lc * LANES, LANES)] = ys[k2][1]
        return carry

    lax.fori_loop(0, rows // SUBLANES, butterfly, 0)

    for k2 in range(FFT_BLOCKS):
        yk = _dot(seq_ref[k2], b_ref[k2].astype(_BF16)).astype(_BF16)
        for a in range(rows // FFT_BLOCKS):
            g_ref[a, pl.ds(FFT_BLOCKS * k2, FFT_BLOCKS), :] = yk[FFT_BLOCKS * a:FFT_BLOCKS * (a + 1), :]

    tile = FFT_BLOCKS * FFT_BLOCKS
    for a in range(rows // FFT_BLOCKS):
        o_ref[0, pl.ds(a * tile, tile), :] = _dot(perm_ref[...], g_ref[a]).astype(o_ref.dtype)


def _fourier_call(z_four):
    b, s, w = z_four.shape
    rows = s // FFT_BLOCKS
    assert rows == FFT_BLOCKS * FFT_BLOCKS and w % GROUP_DIM == 0
    chan, seq_mats, perm = (c.astype(_BF16) for c in _fourier_constants(s))
    blk = pl.BlockSpec((1, s, GROUP_DIM), lambda i, g: (i, 0, g))
    return pl.pallas_call(
        _fourier_kernel,
        grid=(b, w // GROUP_DIM),
        in_specs=[
            blk,
            pl.BlockSpec(chan.shape, lambda i, g: (0, 0)),
            pl.BlockSpec(seq_mats.shape, lambda i, g: (0, 0, 0)),
            pl.BlockSpec(perm.shape, lambda i, g: (0, 0)),
        ],
        out_specs=blk,
        out_shape=jax.ShapeDtypeStruct((b, s, w), _BF16),
        scratch_shapes=[
            pltpu.VMEM((s, 2 * GROUP_DIM), _F32),
            pltpu.VMEM((FFT_BLOCKS, 2 * rows, GROUP_DIM), _F32),
            pltpu.VMEM((rows // FFT_BLOCKS, FFT_BLOCKS * FFT_BLOCKS, GROUP_DIM), _BF16),
        ],
        compiler_params=pltpu.CompilerParams(
            dimension_semantics=("parallel", "parallel"), vmem_limit_bytes=VMEM_LIMIT_BYTES),
        name="fourier_mix",
    )(z_four, chan, seq_mats, perm)


def _softplus(x):
    return jnp.maximum(x, 0.0) + jnp.log1p(jnp.exp(-jnp.abs(x)))


def _lru_kernel(zr_ref, zg_ref, cw_ref, cb_ref, wcat_ref, bcat_ref, lam_ref, o_ref,
                zpad_ref, af_ref, uf_ref, ab_ref, ub_ref, hf_ref, hb_ref, tot_ref, cin_ref):
    seq = zr_ref.shape[1]
    hd = HEAD_DIM
    n_chunks = seq // SCAN_CHUNK
    n_groups = n_chunks // SUBLANES
    pad = SUBLANES

    zpad_ref[pl.ds(0, pad), :] = jnp.zeros((pad, hd), _F32)
    zpad_ref[pl.ds(pad + seq, pad), :] = jnp.zeros((pad, hd), _F32)
    zpad_ref[pl.ds(pad, seq), :] = zr_ref[0].astype(_F32)

    cw = cw_ref[...]
    cb = cb_ref[...]
    rate = LRU_C * _softplus(-lam_ref[0])

    def gates(c, carry):
        r0 = pl.multiple_of(c * SCAN_CHUNK, SCAN_CHUNK)
        p0 = pl.multiple_of(c * SCAN_PITCH, SUBLANES)
        win = zpad_ref[pl.ds(r0, SCAN_CHUNK + 2 * pad), :]
        v = cb
        for k in range(4):
            lo = pad - 2 + k
            v = v + win[lo:lo + SCAN_CHUNK, :] * cw[k:k + 1, :]
        gv = _dot(v.astype(_BF16), wcat_ref[0]) + bcat_ref[0]
        for d, (a_ref, u_ref) in enumerate(((af_ref, uf_ref), (ab_ref, ub_ref))):
            r = _sigmoid(gv[:, (2 * d) * hd:(2 * d + 1) * hd])
            i = _sigmoid(gv[:, (2 * d + 1) * hd:(2 * d + 2) * hd])
            neg_log_a = rate[d:d + 1, :] * r
            a = jnp.exp(-neg_log_a)
            scale = jnp.sqrt(jnp.tanh(neg_log_a) * (1.0 + a * a))
            a_ref[pl.ds(p0, SCAN_CHUNK), :] = a
            u_ref[pl.ds(p0, SCAN_CHUNK), :] = scale * i * v
        return carry

    lax.fori_loop(0, n_chunks, gates, 0)

    def chunk_rows(g, t):
        return pl.ds(g * SUBLANES * SCAN_PITCH + t, SUBLANES, stride=SCAN_PITCH)

    def scan(a_ref, u_ref, h_ref, reverse):
        step_of = (lambda i: SCAN_CHUNK - 1 - i) if reverse else (lambda i: i)

        def totals(i, carry):
            t = step_of(i)
            out = []
            for g in range(n_groups):
                prod, resp = carry[g]
                a = a_ref[chunk_rows(g, t), :]
                u = u_ref[chunk_rows(g, t), :]
                out.append((prod * a, a * resp + u))
            return tuple(out)

        init = tuple((jnp.ones((SUBLANES, hd), _F32), jnp.zeros((SUBLANES, hd), _F32))
                     for _ in range(n_groups))
        tot = lax.fori_loop(0, SCAN_CHUNK, totals, init)
        for g in range(n_groups):
            tot_ref[0, pl.ds(g * SUBLANES, SUBLANES), :] = tot[g][0]
            tot_ref[1, pl.ds(g * SUBLANES, SUBLANES), :] = tot[g][1]

        state = jnp.zeros((1, hd), _F32)
        order = range(n_chunks - 1, -1, -1) if reverse else range(n_chunks)
        for c in order:
            cin_ref[pl.ds(c, 1), :] = state
            state = tot_ref[0, pl.ds(c, 1), :] * state + tot_ref[1, pl.ds(c, 1), :]

        def replay(i, carry):
            t = step_of(i)
            out = []
            for g in range(n_groups):
                a = a_ref[chunk_rows(g, t), :]
                u = u_ref[chunk_rows(g, t), :]
                h = a * carry[g] + u
                h_ref[chunk_rows(g, t), :] = h
                out.append(h)
            return tuple(out)

        lax.fori_loop(0, SCAN_CHUNK, replay,
                      tuple(cin_ref[pl.ds(g * SUBLANES, SUBLANES), :] for g in range(n_groups)))

    scan(af_ref, uf_ref, hf_ref, reverse=False)
    scan(ab_ref, ub_ref, hb_ref, reverse=True)

    def finish(c, carry):
        r0 = pl.multiple_of(c * SCAN_CHUNK, SCAN_CHUNK)
        p0 = pl.multiple_of(c * SCAN_PITCH, SUBLANES)
        h = hf_ref[pl.ds(p0, SCAN_CHUNK), :] + hb_ref[pl.ds(p0, SCAN_CHUNK), :]
        zg = zg_ref[0, pl.ds(r0, SCAN_CHUNK), :].astype(_F32)
        o_ref[0, pl.ds(r0, SCAN_CHUNK), :] = (h * _gelu_tanh(zg)).astype(o_ref.dtype)
        return carry

    lax.fori_loop(0, n_chunks, finish, 0)


def _lru_call(z_rec, z_gelu, conv_w, conv_b, wcat, bcat, lam):
    b, s, w = z_rec.shape
    heads = w // HEAD_DIM
    n_chunks = s // SCAN_CHUNK
    assert s % (SCAN_CHUNK * SUBLANES) == 0
    blk = pl.BlockSpec((1, s, HEAD_DIM), lambda i, h: (i, 0, h))
    pitched = pltpu.VMEM((n_chunks * SCAN_PITCH, HEAD_DIM), _F32)
    return pl.pallas_call(
        _lru_kernel,
        grid=(b, heads),
        in_specs=[
            blk,
            blk,
            pl.BlockSpec((conv_w.shape[0], HEAD_DIM), lambda i, h: (0, h)),
            pl.BlockSpec((1, HEAD_DIM), lambda i, h: (0, h)),
            pl.BlockSpec((1, HEAD_DIM, 4 * HEAD_DIM), lambda i, h: (h, 0, 0)),
            pl.BlockSpec((1, 1, 4 * HEAD_DIM), lambda i, h: (h, 0, 0)),
            pl.BlockSpec((1, 2, HEAD_DIM), lambda i, h: (h, 0, 0)),
        ],
        out_specs=blk,
        out_shape=jax.ShapeDtypeStruct((b, s, w), _BF16),
        scratch_shapes=[
            pltpu.VMEM((s + 2 * SUBLANES, HEAD_DIM), _F32),
            pitched, pitched, pitched, pitched, pitched, pitched,
            pltpu.VMEM((2, n_chunks, HEAD_DIM), _F32),
            pltpu.VMEM((n_chunks, HEAD_DIM), _F32),
        ],
        compiler_params=pltpu.CompilerParams(
            dimension_semantics=("parallel", "parallel"), vmem_limit_bytes=VMEM_LIMIT_BYTES),
        name="rglru",
    )(z_rec, z_gelu, conv_w, conv_b, wcat, bcat, lam)


def _merge_kernel(x_ref, yf_ref, yg_ref, gates_ref, pa_ref, pb_ref, wo_ref, o_ref):
    d = x_ref.shape[1]
    y_a = _dot(yf_ref[...], pa_ref[...])
    y_b = _dot(yg_ref[...], pb_ref[...])
    merged = gates_ref[:, :d].astype(_F32) * y_a + gates_ref[:, d:].astype(_F32) * y_b
    o_ref[...] = x_ref[...] + _dot(merged.astype(_BF16), wo_ref[...])


def _merge_call(x, y_four, y_rec, gates, proj_a, proj_b, w_out, *, tm=256):
    t, d = x.shape
    resident = functools.partial(pl.BlockSpec, index_map=lambda i: (0, 0), pipeline_mode=pl.Buffered(1))
    return pl.pallas_call(
        _merge_kernel,
        grid=(t // tm,),
        in_specs=[
            pl.BlockSpec((tm, d), lambda i: (i, 0)),
            pl.BlockSpec((tm, y_four.shape[1]), lambda i: (i, 0)),
            pl.BlockSpec((tm, y_rec.shape[1]), lambda i: (i, 0)),
            pl.BlockSpec((tm, gates.shape[1]), lambda i: (i, 0)),
            resident(proj_a.shape),
            resident(proj_b.shape),
            resident(w_out.shape),
        ],
        out_specs=pl.BlockSpec((tm, d), lambda i: (i, 0)),
        out_shape=jax.ShapeDtypeStruct((t, d), _F32),
        compiler_params=pltpu.CompilerParams(
            dimension_semantics=("parallel",), vmem_limit_bytes=VMEM_LIMIT_BYTES),
        name="merge",
    )(x, y_four, y_rec, gates, proj_a, proj_b, w_out)


def kernel(x, ffn1_norm, ffn1_w_gate, ffn1_w_up, ffn1_w_down, mix_norm, w_in, b_gates, conv_w, conv_b, lru_wa, lru_ba, lru_wx, lru_bx, lru_lambda, proj_a, proj_b, w_out, ffn2_norm, ffn2_w_gate, ffn2_w_up, ffn2_w_down, final_norm):
    b, s, d = x.shape
    assert ffn1_norm.shape[0] == 1, "single-layer problem"
    bf = lambda w: w.astype(_BF16)
    x1, u = _ffn_call(x.reshape(b * s, d), ffn1_norm, bf(ffn1_w_gate[0]), bf(ffn1_w_up[0]), bf(ffn1_w_down[0]),
                      mix_norm, emit_residual=True)
    z_four, z_rec, z_gelu, gates = _inproj_call(u, bf(w_in[0]), b_gates.reshape(1, GATE_WIDTH))

    y_four = _fourier_call(z_four.reshape(b, s, FOURIER_WIDTH))

    wcat = jnp.concatenate([lru_wa[0, 0], lru_wx[0, 0], lru_wa[0, 1], lru_wx[0, 1]], axis=-1)
    bcat = jnp.concatenate([lru_ba[0, 0], lru_bx[0, 0], lru_ba[0, 1], lru_bx[0, 1]], axis=-1)
    lam = jnp.transpose(lru_lambda[0], (1, 0, 2))
    y_rec = _lru_call(z_rec.reshape(b, s, LRU_WIDTH), z_gelu.reshape(b, s, LRU_WIDTH),
                      conv_w[0], conv_b, bf(wcat), bcat[:, None, :], lam)

    x2 = _merge_call(x1, y_four.reshape(b * s, FOURIER_WIDTH), y_rec.reshape(b * s, LRU_WIDTH), gates,
                     bf(proj_a[0]), bf(proj_b[0]), bf(w_out[0]))
    out = _ffn_call(x2, ffn2_norm, bf(ffn2_w_gate[0]), bf(ffn2_w_up[0]), bf(ffn2_w_down[0]),
                    final_norm[None], emit_residual=False)
    return out.reshape(b, s, d)
```

```python
import functools
import math

import numpy as np
import jax
import jax.numpy as jnp
from jax import lax
from jax.experimental import pallas as pl
from jax.experimental.pallas import tpu as pltpu

D_MODEL = 2048
D_FF = 5632
FOURIER_WIDTH = 1024
FOURIER_GROUPS = 4
GROUP_DIM = FOURIER_WIDTH // FOURIER_GROUPS
LRU_WIDTH = 1024
LRU_HEADS = 8
HEAD_DIM = LRU_WIDTH // LRU_HEADS
LRU_C = 8.0
RMS_EPS = 1e-6
FFN_RES_SCALE = 0.5
GATE_WIDTH = 2 * D_MODEL
IN_WIDTH = FOURIER_WIDTH + 2 * LRU_WIDTH + GATE_WIDTH

VMEM_LIMIT_BYTES = 56 * 1024 * 1024
SUBLANES = 8
LANES = 128

FFT_BLOCKS = 16
SCAN_CHUNK = 128
SCAN_PITCH = SCAN_CHUNK + SUBLANES
CONV_PITCH = SCAN_CHUNK + 3 * SUBLANES
GATE_STEPS = 8
SCAN_UNROLL = 4
LOG2_E = 1.0 / math.log(2.0)
LN_2 = math.log(2.0)

_BF16 = jnp.bfloat16
_F32 = jnp.float32


def _dot(a, b):
    return jnp.dot(a, b, preferred_element_type=_F32)


def _sigmoid(x):
    return 0.5 * (1.0 + jnp.tanh(0.5 * x))


def _rms_norm(x, gain):
    ms = jnp.mean(x * x, axis=-1, keepdims=True)
    return x * lax.rsqrt(ms + RMS_EPS) * gain


def _gelu_tanh(x):
    return 0.5 * x * (1.0 + jnp.tanh(math.sqrt(2.0 / math.pi) * (x + 0.044715 * (x * x * x))))


def _ffn_kernel(x_ref, gain_ref, wg_ref, wu_ref, wd_ref, post_ref, *refs, emit_residual):
    if emit_residual:
        res_ref, normed_ref, h_ref, acc_ref = refs
    else:
        normed_ref, h_ref, acc_ref = refs
    j = pl.program_id(1)

    @pl.when(j == 0)
    def _():
        h_ref[...] = _rms_norm(x_ref[...], gain_ref[...]).astype(_BF16)
        acc_ref[...] = jnp.zeros_like(acc_ref)

    h = h_ref[...]
    g = _dot(h, wg_ref[...])
    u = _dot(h, wu_ref[...])
    act = (g * _sigmoid(g) * u).astype(_BF16)
    acc_ref[...] += _dot(act, wd_ref[...])

    @pl.when(j == pl.num_programs(1) - 1)
    def _():
        y = x_ref[...] + FFN_RES_SCALE * acc_ref[...]
        if emit_residual:
            res_ref[...] = y
        normed_ref[...] = _rms_norm(y, post_ref[...]).astype(normed_ref.dtype)


def _ffn_call(x, gain, wg, wu, wd, post_gain, *, emit_residual, tm=512, tf=512):
    t, d = x.shape
    f = wg.shape[1]
    grid = (t // tm, f // tf)
    row_spec = pl.BlockSpec((tm, d), lambda i, j: (i, 0))
    vec_spec = pl.BlockSpec((1, d), lambda i, j: (0, 0))
    in_specs = [
        row_spec,
        vec_spec,
        pl.BlockSpec((d, tf), lambda i, j: (0, j)),
        pl.BlockSpec((d, tf), lambda i, j: (0, j)),
        pl.BlockSpec((tf, d), lambda i, j: (j, 0)),
        vec_spec,
    ]
    if emit_residual:
        out_shape = (jax.ShapeDtypeStruct((t, d), _F32), jax.ShapeDtypeStruct((t, d), _BF16))
        out_specs = (row_spec, row_spec)
    else:
        out_shape = jax.ShapeDtypeStruct((t, d), _F32)
        out_specs = row_spec
    return pl.pallas_call(
        functools.partial(_ffn_kernel, emit_residual=emit_residual),
        grid=grid,
        in_specs=in_specs,
        out_specs=out_specs,
        out_shape=out_shape,
        scratch_shapes=[pltpu.VMEM((tm, d), _BF16), pltpu.VMEM((tm, d), _F32)],
        compiler_params=pltpu.CompilerParams(
            dimension_semantics=("parallel", "arbitrary"), vmem_limit_bytes=VMEM_LIMIT_BYTES),
        name="ffn_residual" if emit_residual else "ffn_final",
    )(x, gain, wg, wu, wd, post_gain)


def _inproj_kernel(u_ref, w_ref, bg_ref, zf_ref, zr_ref, zg_ref, gates_ref):
    j = pl.program_id(1)
    for col, ref in enumerate((zf_ref, zr_ref, zg_ref)):
        @pl.when(j == col)
        def _(ref=ref):
            ref[...] = _dot(u_ref[...], w_ref[...]).astype(_BF16)

    @pl.when(j >= 3)
    def _():
        gates_ref[...] = _sigmoid(_dot(u_ref[...], w_ref[...]) + bg_ref[...]).astype(_BF16)


def _inproj_call(u, w_in, b_gates, *, tm=1024, tn=1024):
    t, d = u.shape
    n_side = (FOURIER_WIDTH + 2 * LRU_WIDTH) // tn
    grid = (t // tm, IN_WIDTH // tn)
    side_spec = pl.BlockSpec((tm, tn), lambda i, j: (i, 0))
    gate_col = lambda i, j: jnp.maximum(j - n_side, 0)
    return pl.pallas_call(
        _inproj_kernel,
        grid=grid,
        in_specs=[
            pl.BlockSpec((tm, d), lambda i, j: (i, 0)),
            pl.BlockSpec((d, tn), lambda i, j: (0, j)),
            pl.BlockSpec((1, tn), lambda i, j: (0, gate_col(i, j))),
        ],
        out_specs=(side_spec, side_spec, side_spec,
                   pl.BlockSpec((tm, tn), lambda i, j: (i, gate_col(i, j)))),
        out_shape=(
            jax.ShapeDtypeStruct((t, FOURIER_WIDTH), _BF16),
            jax.ShapeDtypeStruct((t, LRU_WIDTH), _BF16),
            jax.ShapeDtypeStruct((t, LRU_WIDTH), _BF16),
            jax.ShapeDtypeStruct((t, GATE_WIDTH), _BF16),
        ),
        compiler_params=pltpu.CompilerParams(
            dimension_semantics=("parallel", "arbitrary"), vmem_limit_bytes=VMEM_LIMIT_BYTES),
        name="inproj",
    )(u, w_in, b_gates)


def _fourier_constants(seq):
    rows = seq // FFT_BLOCKS
    c = np.arange(GROUP_DIM)
    ang_c = 2.0 * np.pi * np.outer(c, c) / GROUP_DIM
    chan = np.concatenate([np.cos(ang_c), -np.sin(ang_c)], axis=1) / math.sqrt(GROUP_DIM)
    n1 = np.arange(rows)
    k1 = np.arange(rows)
    mats = []
    for k2 in range(FFT_BLOCKS):
        ang = 2.0 * np.pi * np.outer(FFT_BLOCKS * k1 + k2, n1) / seq
        mats.append(np.concatenate([np.cos(ang), np.sin(ang)], axis=1) / math.sqrt(seq))
    seq_mats = np.stack(mats)
    tile = FFT_BLOCKS * FFT_BLOCKS
    rho = np.arange(tile)
    perm = np.zeros((tile, tile), np.float32)
    perm[rho, FFT_BLOCKS * (rho % FFT_BLOCKS) + rho // FFT_BLOCKS] = 1.0
    return (jnp.asarray(chan, _F32), jnp.asarray(seq_mats, _F32), jnp.asarray(perm, _F32))


def _fft_across_blocks(xs):
    n = len(xs)
    if n == 1:
        return xs
    ev = _fft_across_blocks(xs[0::2])
    od = _fft_across_blocks(xs[1::2])
    out = [None] * n
    for k in range(n // 2):
        o_r, o_i = od[k]
        e_r, e_i = ev[k]
        if k == 0:
            t_r, t_i = o_r, o_i
        elif 4 * k == n:
            out[k] = (e_r + o_i, e_i - o_r)
            out[k + n // 2] = (e_r - o_i, e_i + o_r)
            continue
        else:
            c = math.cos(2.0 * math.pi * k / n)
            s = -math.sin(2.0 * math.pi * k / n)
            t_r = c * o_r - s * o_i
            t_i = c * o_i + s * o_r
        out[k] = (e_r + t_r, e_i + t_i)
        out[k + n // 2] = (e_r - t_r, e_i - t_i)
    return out


def _fourier_kernel(z_ref, chan_ref, seq_ref, perm_ref, o_ref, w_ref, b_ref, g_ref):
    seq = z_ref.shape[1]
    rows = seq // FFT_BLOCKS
    gd = GROUP_DIM

    for blk in range(FFT_BLOCKS):
        sl = pl.ds(blk * rows, rows)
        w_ref[sl, :] = _dot(z_ref[0, sl, :], chan_ref[...])

    def butterfly(r, carry):
        r0 = pl.multiple_of(r * SUBLANES, SUBLANES)
        for lc in range(gd // LANES):
            re_l = pl.ds(lc * LANES, LANES)
            im_l = pl.ds(gd + lc * LANES, LANES)
            xs = [(w_ref[pl.ds(blk * rows + r0, SUBLANES), re_l],
                   w_ref[pl.ds(blk * rows + r0, SUBLANES), im_l]) for blk in range(FFT_BLOCKS)]
            ys = _fft_across_blocks(xs)
            for k2 in range(FFT_BLOCKS):
                b_ref[k2, pl.ds(r0, SUBLANES), pl.ds(lc * LANES, LANES)] = ys[k2][0]
                b_ref[k2, pl.ds(rows + r0, SUBLANES), pl.ds(lc * LANES, LANES)] = ys[k2][1]
        return carry

    lax.fori_loop(0, rows // SUBLANES, butterfly, 0)

    for k2 in range(FFT_BLOCKS):
        yk = _dot(seq_ref[k2], b_ref[k2].astype(_BF16)).astype(_BF16)
        for a in range(rows // FFT_BLOCKS):
            g_ref[a, pl.ds(FFT_BLOCKS * k2, FFT_BLOCKS), :] = yk[FFT_BLOCKS * a:FFT_BLOCKS * (a + 1), :]

    tile = FFT_BLOCKS * FFT_BLOCKS
    for a in range(rows // FFT_BLOCKS):
        o_ref[0, pl.ds(a * tile, tile), :] = _dot(perm_ref[...], g_ref[a]).astype(o_ref.dtype)


def _fourier_call(z_four):
    b, s, w = z_four.shape
    rows = s // FFT_BLOCKS
    assert rows == FFT_BLOCKS * FFT_BLOCKS and w % GROUP_DIM == 0
    chan, seq_mats, perm = (c.astype(_BF16) for c in _fourier_constants(s))
    blk = pl.BlockSpec((1, s, GROUP_DIM), lambda i, g: (i, 0, g))
    return pl.pallas_call(
        _fourier_kernel,
        grid=(b, w // GROUP_DIM),
        in_specs=[
            blk,
            pl.BlockSpec(chan.shape, lambda i, g: (0, 0)),
            pl.BlockSpec(seq_mats.shape, lambda i, g: (0, 0, 0)),
            pl.BlockSpec(perm.shape, lambda i, g: (0, 0)),
        ],
        out_specs=blk,
        out_shape=jax.ShapeDtypeStruct((b, s, w), _BF16),
        scratch_shapes=[
            pltpu.VMEM((s, 2 * GROUP_DIM), _F32),
            pltpu.VMEM((FFT_BLOCKS, 2 * rows, GROUP_DIM), _F32),
            pltpu.VMEM((rows // FFT_BLOCKS, FFT_BLOCKS * FFT_BLOCKS, GROUP_DIM), _BF16),
        ],
        compiler_params=pltpu.CompilerParams(
            dimension_semantics=("parallel", "parallel"), vmem_limit_bytes=VMEM_LIMIT_BYTES),
        name="fourier_mix",
    )(z_four, chan, seq_mats, perm)


def _softplus(x):
    return jnp.maximum(x, 0.0) + jnp.log1p(jnp.exp(-jnp.abs(x)))


def _lru_kernel(zr_ref, zg_ref, cw_ref, cb_ref, wcat_ref, bcat_ref, lam_ref, o_ref,
                zpad_ref, af_ref, uf_ref, ab_ref, ub_ref, hf_ref, hb_ref, tot_ref, cin_ref):
    seq = zr_ref.shape[1]
    hd = HEAD_DIM
    n_chunks = seq // SCAN_CHUNK
    n_groups = n_chunks // SUBLANES
    halo = SUBLANES

    def fill(c, carry):
        r0 = pl.multiple_of(c * SCAN_CHUNK, SCAN_CHUNK)
        p0 = pl.multiple_of(c * CONV_PITCH, SUBLANES)
        zpad_ref[pl.ds(p0 + halo, SCAN_CHUNK), :] = zr_ref[0, pl.ds(r0, SCAN_CHUNK), :].astype(_F32)
        return carry

    lax.fori_loop(0, n_chunks, fill, 0)
    for c in range(n_chunks):
        before = (zpad_ref[pl.ds((c - 1) * CONV_PITCH + SCAN_CHUNK, halo), :] if c > 0
                  else jnp.zeros((halo, hd), _F32))
        after = (zpad_ref[pl.ds((c + 1) * CONV_PITCH + halo, halo), :] if c < n_chunks - 1
                 else jnp.zeros((halo, hd), _F32))
        zpad_ref[pl.ds(c * CONV_PITCH, halo), :] = before
        zpad_ref[pl.ds(c * CONV_PITCH + halo + SCAN_CHUNK, halo), :] = after

    cw = 0.5 * cw_ref[...]
    cb = 0.5 * cb_ref[...]
    half_rate = (-0.5 * LRU_C * LOG2_E) * _softplus(-lam_ref[0])
    directions = ((af_ref, uf_ref, hf_ref), (ab_ref, ub_ref, hb_ref))

    def gates(tb, carry):
        t0 = tb * GATE_STEPS
        taps = [zpad_ref[pl.ds(halo - 2 + t0 + m, n_chunks, stride=CONV_PITCH), :]
                for m in range(GATE_STEPS + 3)]
        hv = jnp.concatenate(
            [cb + sum(taps[j + k] * cw[k:k + 1, :] for k in range(4)) for j in range(GATE_STEPS)], axis=0)
        th = jnp.tanh(_dot(hv.astype(_BF16), wcat_ref[0]) + bcat_ref[0])
        rows = pl.ds(pl.multiple_of(t0 * n_chunks, GATE_STEPS * n_chunks), GATE_STEPS * n_chunks)
        for d, (a_ref, u_ref, _) in enumerate(directions):
            th_r = th[:, (2 * d) * hd:(2 * d + 1) * hd]
            th_i = th[:, (2 * d + 1) * hd:(2 * d + 2) * hd]
            hr = half_rate[d:d + 1, :]
            log2_a = hr + hr * th_r
            a = jnp.exp2(log2_a)
            x = jnp.tanh(log2_a * (-LN_2)) * (1.0 + a * a)
            scale = jnp.where(x > 0.0, x * lax.rsqrt(x), 0.0)
            a_ref[rows, :] = a
            u_ref[rows, :] = scale * (1.0 + th_i) * hv
        return carry

    lax.fori_loop(0, SCAN_CHUNK // GATE_STEPS, gates, 0, unroll=2)

    def step_rows(d, g, i):
        t = i if d == 0 else SCAN_CHUNK - 1 - i
        return pl.ds(pl.multiple_of(t * n_chunks, n_chunks) + g * SUBLANES, SUBLANES)

    def chunk_rows(d, g, i):
        t = i if d == 0 else SCAN_CHUNK - 1 - i
        return pl.ds(g * SUBLANES * SCAN_PITCH + t, SUBLANES, stride=SCAN_PITCH)

    lanes = [(d, g) for d in range(2) for g in range(n_groups)]

    def totals(i, carry):
        out = []
        for (d, g), (prod, resp) in zip(lanes, carry):
            a = directions[d][0][step_rows(d, g, i), :]
            u = directions[d][1][step_rows(d, g, i), :]
            out.append((prod * a, a * resp + u))
        return tuple(out)

    init = tuple((jnp.ones((SUBLANES, hd), _F32), jnp.zeros((SUBLANES, hd), _F32)) for _ in lanes)
    tot = lax.fori_loop(0, SCAN_CHUNK, totals, init, unroll=SCAN_UNROLL)
    for (d, g), (prod, resp) in zip(lanes, tot):
        tot_ref[2 * d, pl.ds(g * SUBLANES, SUBLANES), :] = prod
        tot_ref[2 * d + 1, pl.ds(g * SUBLANES, SUBLANES), :] = resp

    for d in range(2):
        state = jnp.zeros((1, hd), _F32)
        for c in (range(n_chunks) if d == 0 else range(n_chunks - 1, -1, -1)):
            cin_ref[d, pl.ds(c, 1), :] = state
            state = tot_ref[2 * d, pl.ds(c, 1), :] * state + tot_ref[2 * d + 1, pl.ds(c, 1), :]

    def replay(i, carry):
        out = []
        for (d, g), h_in in zip(lanes, carry):
            a = directions[d][0][step_rows(d, g, i), :]
            u = directions[d][1][step_rows(d, g, i), :]
            h = a * h_in + u
            directions[d][2][chunk_rows(d, g, i), :] = h
            out.append(h)
        return tuple(out)

    lax.fori_loop(0, SCAN_CHUNK, replay,
                  tuple(cin_ref[d, pl.ds(g * SUBLANES, SUBLANES), :] for d, g in lanes), unroll=SCAN_UNROLL)

    def finish(c, carry):
        r0 = pl.multiple_of(c * SCAN_CHUNK, SCAN_CHUNK)
        p0 = pl.multiple_of(c * SCAN_PITCH, SUBLANES)
        h = hf_ref[pl.ds(p0, SCAN_CHUNK), :] + hb_ref[pl.ds(p0, SCAN_CHUNK), :]
        zg = zg_ref[0, pl.ds(r0, SCAN_CHUNK), :].astype(_F32)
        o_ref[0, pl.ds(r0, SCAN_CHUNK), :] = (h * _gelu_tanh(zg)).astype(o_ref.dtype)
        return carry

    lax.fori_loop(0, n_chunks, finish, 0)


def _lru_call(z_rec, z_gelu, conv_w, conv_b, wcat, bcat, lam):
    b, s, w = z_rec.shape
    heads = w // HEAD_DIM
    n_chunks = s // SCAN_CHUNK
    assert s % (SCAN_CHUNK * SUBLANES) == 0
    blk = pl.BlockSpec((1, s, HEAD_DIM), lambda i, h: (i, 0, h))
    pitched = pltpu.VMEM((n_chunks * SCAN_PITCH, HEAD_DIM), _F32)
    time_major = pltpu.VMEM((s, HEAD_DIM), _F32)
    return pl.pallas_call(
        _lru_kernel,
        grid=(b, heads),
        in_specs=[
            blk,
            blk,
            pl.BlockSpec((conv_w.shape[0], HEAD_DIM), lambda i, h: (0, h)),
            pl.BlockSpec((1, HEAD_DIM), lambda i, h: (0, h)),
            pl.BlockSpec((1, HEAD_DIM, 4 * HEAD_DIM), lambda i, h: (h, 0, 0)),
            pl.BlockSpec((1, 1, 4 * HEAD_DIM), lambda i, h: (h, 0, 0)),
            pl.BlockSpec((1, 2, HEAD_DIM), lambda i, h: (h, 0, 0)),
        ],
        out_specs=blk,
        out_shape=jax.ShapeDtypeStruct((b, s, w), _BF16),
        scratch_shapes=[
            pltpu.VMEM((n_chunks * CONV_PITCH, HEAD_DIM), _F32),
            time_major, time_major, time_major, time_major, pitched, pitched,
            pltpu.VMEM((4, n_chunks, HEAD_DIM), _F32),
            pltpu.VMEM((2, n_chunks, HEAD_DIM), _F32),
        ],
        compiler_params=pltpu.CompilerParams(
            dimension_semantics=("parallel", "parallel"), vmem_limit_bytes=VMEM_LIMIT_BYTES),
        name="rglru",
    )(z_rec, z_gelu, conv_w, conv_b, wcat, bcat, lam)


def _merge_kernel(x_ref, yf_ref, yg_ref, gates_ref, pa_ref, pb_ref, wo_ref, o_ref):
    d = x_ref.shape[1]
    y_a = _dot(yf_ref[...], pa_ref[...])
    y_b = _dot(yg_ref[...], pb_ref[...])
    merged = gates_ref[:, :d].astype(_F32) * y_a + gates_ref[:, d:].astype(_F32) * y_b
    o_ref[...] = x_ref[...] + _dot(merged.astype(_BF16), wo_ref[...])


def _merge_call(x, y_four, y_rec, gates, proj_a, proj_b, w_out, *, tm=256):
    t, d = x.shape
    resident = functools.partial(pl.BlockSpec, index_map=lambda i: (0, 0), pipeline_mode=pl.Buffered(1))
    return pl.pallas_call(
        _merge_kernel,
        grid=(t // tm,),
        in_specs=[
            pl.BlockSpec((tm, d), lambda i: (i, 0)),
            pl.BlockSpec((tm, y_four.shape[1]), lambda i: (i, 0)),
            pl.BlockSpec((tm, y_rec.shape[1]), lambda i: (i, 0)),
            pl.BlockSpec((tm, gates.shape[1]), lambda i: (i, 0)),
            resident(proj_a.shape),
            resident(proj_b.shape),
            resident(w_out.shape),
        ],
        out_specs=pl.BlockSpec((tm, d), lambda i: (i, 0)),
        out_shape=jax.ShapeDtypeStruct((t, d), _F32),
        compiler_params=pltpu.CompilerParams(
            dimension_semantics=("parallel",), vmem_limit_bytes=VMEM_LIMIT_BYTES),
        name="merge",
    )(x, y_four, y_rec, gates, proj_a, proj_b, w_out)


def kernel(x, ffn1_norm, ffn1_w_gate, ffn1_w_up, ffn1_w_down, mix_norm, w_in, b_gates, conv_w, conv_b, lru_wa, lru_ba, lru_wx, lru_bx, lru_lambda, proj_a, proj_b, w_out, ffn2_norm, ffn2_w_gate, ffn2_w_up, ffn2_w_down, final_norm):
    b, s, d = x.shape
    assert ffn1_norm.shape[0] == 1, "single-layer problem"
    bf = lambda w: w.astype(_BF16)
    x1, u = _ffn_call(x.reshape(b * s, d), ffn1_norm, bf(ffn1_w_gate[0]), bf(ffn1_w_up[0]), bf(ffn1_w_down[0]),
                      mix_norm, emit_residual=True)
    z_four, z_rec, z_gelu, gates = _inproj_call(u, bf(w_in[0]), b_gates.reshape(1, GATE_WIDTH))

    y_four = _fourier_call(z_four.reshape(b, s, FOURIER_WIDTH))

    wcat = jnp.concatenate([lru_wa[0, 0], lru_wx[0, 0], lru_wa[0, 1], lru_wx[0, 1]], axis=-1)
    bcat = 0.5 * jnp.concatenate([lru_ba[0, 0], lru_bx[0, 0], lru_ba[0, 1], lru_bx[0, 1]], axis=-1)
    lam = jnp.transpose(lru_lambda[0], (1, 0, 2))
    y_rec = _lru_call(z_rec.reshape(b, s, LRU_WIDTH), z_gelu.reshape(b, s, LRU_WIDTH),
                      conv_w[0], conv_b, bf(wcat), bcat[:, None, :], lam)

    x2 = _merge_call(x1, y_four.reshape(b * s, FOURIER_WIDTH), y_rec.reshape(b * s, LRU_WIDTH), gates,
                     bf(proj_a[0]), bf(proj_b[0]), bf(w_out[0]))
    out = _ffn_call(x2, ffn2_norm, bf(ffn2_w_gate[0]), bf(ffn2_w_up[0]), bf(ffn2_w_down[0]),
                    final_norm[None], emit_residual=False)
    return out.reshape(b, s, d)
```

```python
import functools
import math

import numpy as np
import jax
import jax.numpy as jnp
from jax import lax
from jax.experimental import pallas as pl
from jax.experimental.pallas import tpu as pltpu

D_MODEL = 2048
D_FF = 5632
FOURIER_WIDTH = 1024
FOURIER_GROUPS = 4
GROUP_DIM = FOURIER_WIDTH // FOURIER_GROUPS
LRU_WIDTH = 1024
LRU_HEADS = 8
HEAD_DIM = LRU_WIDTH // LRU_HEADS
LRU_C = 8.0
RMS_EPS = 1e-6
FFN_RES_SCALE = 0.5
GATE_WIDTH = 2 * D_MODEL
IN_WIDTH = FOURIER_WIDTH + 2 * LRU_WIDTH + GATE_WIDTH

VMEM_LIMIT_BYTES = 56 * 1024 * 1024
SUBLANES = 8
LANES = 128
CAST_ROWS = 2 * SUBLANES

FFT_BLOCKS = 16
SCAN_CHUNK = 128
SCAN_PITCH = SCAN_CHUNK + SUBLANES
CONV_PITCH = SCAN_CHUNK + 3 * SUBLANES
GATE_STEPS = 8
SCAN_UNROLL = 4
LOG2_E = 1.0 / math.log(2.0)
LN_2 = math.log(2.0)

_BF16 = jnp.bfloat16
_F32 = jnp.float32


def _dot(a, b):
    return jnp.dot(a, b, preferred_element_type=_F32)


def _sigmoid(x):
    return 0.5 * (1.0 + jnp.tanh(0.5 * x))


def _rms_norm(x, gain):
    ms = jnp.mean(x * x, axis=-1, keepdims=True)
    return x * lax.rsqrt(ms + RMS_EPS) * gain


def _gelu_tanh(x):
    return 0.5 * x * (1.0 + jnp.tanh(math.sqrt(2.0 / math.pi) * (x + 0.044715 * (x * x * x))))


def _ride_along_casts(weights, grid):
    n_steps = grid[0] * grid[1]
    specs, shapes = [], []
    for w in weights:
        rows, cols = w.shape
        block_rows = next(r for r in range(CAST_ROWS, rows + 1, CAST_ROWS)
                          if rows % r == 0 and rows // r <= n_steps)
        n_blocks = rows // block_rows
        specs.append(pl.BlockSpec(
            (block_rows, cols),
            lambda i, j, n_blocks=n_blocks: ((i * grid[1] + j) * n_blocks // n_steps, 0)))
        shapes.append(jax.ShapeDtypeStruct((rows, cols), _BF16))
    return specs, shapes


def _cast_blocks(srcs, dsts):
    for src, dst in zip(srcs, dsts):
        dst[...] = src[...].astype(_BF16)


def _ffn_kernel(*refs, emit_residual, n_casts):
    x_ref, gain_ref, wg_ref, wu_ref, wd_ref, post_ref = refs[:6]
    cast_in = refs[6:6 + n_casts]
    outs = refs[6 + n_casts:]
    if emit_residual:
        res_ref, normed_ref = outs[:2]
        outs = outs[2:]
    else:
        normed_ref = outs[0]
        outs = outs[1:]
    cast_out = outs[:n_casts]
    h_ref, acc_ref = outs[n_casts:]
    j = pl.program_id(1)

    @pl.when(j == 0)
    def _():
        h_ref[...] = _rms_norm(x_ref[...], gain_ref[...]).astype(_BF16)
        acc_ref[...] = jnp.zeros_like(acc_ref)

    h = h_ref[...]
    g = _dot(h, wg_ref[...])
    u = _dot(h, wu_ref[...])
    act = (g * _sigmoid(g) * u).astype(_BF16)
    acc_ref[...] += _dot(act, wd_ref[...])

    _cast_blocks(cast_in, cast_out)

    @pl.when(j == pl.num_programs(1) - 1)
    def _():
        y = x_ref[...] + FFN_RES_SCALE * acc_ref[...]
        if emit_residual:
            res_ref[...] = y
        normed_ref[...] = _rms_norm(y, post_ref[...]).astype(normed_ref.dtype)


def _ffn_call(x, gain, wg, wu, wd, post_gain, *, emit_residual, casts=(), tm=512, tf=512):
    t, d = x.shape
    f = wg.shape[1]
    grid = (t // tm, f // tf)
    row_spec = pl.BlockSpec((tm, d), lambda i, j: (i, 0))
    vec_spec = pl.BlockSpec((1, d), lambda i, j: (0, 0))
    in_specs = [
        row_spec,
        vec_spec,
        pl.BlockSpec((d, tf), lambda i, j: (0, j)),
        pl.BlockSpec((d, tf), lambda i, j: (0, j)),
        pl.BlockSpec((tf, d), lambda i, j: (j, 0)),
        vec_spec,
    ]
    out_shape = [jax.ShapeDtypeStruct((t, d), _BF16 if emit_residual else _F32)]
    out_specs = [row_spec]
    if emit_residual:
        out_shape.insert(0, jax.ShapeDtypeStruct((t, d), _F32))
        out_specs.insert(0, row_spec)
    cast_specs, cast_shapes = _ride_along_casts(casts, grid)
    in_specs += cast_specs
    out_specs += cast_specs
    out_shape += cast_shapes
    return pl.pallas_call(
        functools.partial(_ffn_kernel, emit_residual=emit_residual, n_casts=len(casts)),
        grid=grid,
        in_specs=in_specs,
        out_specs=tuple(out_specs),
        out_shape=tuple(out_shape),
        scratch_shapes=[pltpu.VMEM((tm, d), _BF16), pltpu.VMEM((tm, d), _F32)],
        compiler_params=pltpu.CompilerParams(
            dimension_semantics=("arbitrary", "arbitrary"), vmem_limit_bytes=VMEM_LIMIT_BYTES),
        name="ffn_residual" if emit_residual else "ffn_final",
    )(x, gain, wg, wu, wd, post_gain, *casts)


def _inproj_kernel(*refs, n_casts):
    u_ref, w_ref, bg_ref = refs[:3]
    cast_in = refs[3:3 + n_casts]
    zf_ref, zr_ref, zg_ref, gates_ref = refs[3 + n_casts:7 + n_casts]
    cast_out = refs[7 + n_casts:]
    j = pl.program_id(1)
    for col, ref in enumerate((zf_ref, zr_ref, zg_ref)):
        @pl.when(j == col)
        def _(ref=ref):
            ref[...] = _dot(u_ref[...], w_ref[...]).astype(_BF16)
            _cast_blocks(cast_in, cast_out)

    @pl.when(j >= 3)
    def _():
        gates_ref[...] = _sigmoid(_dot(u_ref[...], w_ref[...]) + bg_ref[...]).astype(_BF16)
        _cast_blocks(cast_in, cast_out)


def _inproj_call(u, w_in, b_gates, *, casts=(), tm=1024, tn=1024):
    t, d = u.shape
    n_side = (FOURIER_WIDTH + 2 * LRU_WIDTH) // tn
    grid = (t // tm, IN_WIDTH // tn)
    side_spec = pl.BlockSpec((tm, tn), lambda i, j: (i, 0))
    gate_col = lambda i, j: jnp.maximum(j - n_side, 0)
    cast_specs, cast_shapes = _ride_along_casts(casts, grid)
    return pl.pallas_call(
        functools.partial(_inproj_kernel, n_casts=len(casts)),
        grid=grid,
        in_specs=[
            pl.BlockSpec((tm, d), lambda i, j: (i, 0)),
            pl.BlockSpec((d, tn), lambda i, j: (0, j)),
            pl.BlockSpec((1, tn), lambda i, j: (0, gate_col(i, j))),
        ] + cast_specs,
        out_specs=(side_spec, side_spec, side_spec,
                   pl.BlockSpec((tm, tn), lambda i, j: (i, gate_col(i, j))), *cast_specs),
        out_shape=(
            jax.ShapeDtypeStruct((t, FOURIER_WIDTH), _BF16),
            jax.ShapeDtypeStruct((t, LRU_WIDTH), _BF16),
            jax.ShapeDtypeStruct((t, LRU_WIDTH), _BF16),
            jax.ShapeDtypeStruct((t, GATE_WIDTH), _BF16),
            *cast_shapes,
        ),
        compiler_params=pltpu.CompilerParams(
            dimension_semantics=("arbitrary", "arbitrary"), vmem_limit_bytes=VMEM_LIMIT_BYTES),
        name="inproj",
    )(u, w_in, b_gates, *casts)


def _fourier_constants(seq):
    rows = seq // FFT_BLOCKS
    c = np.arange(GROUP_DIM)
    ang_c = 2.0 * np.pi * np.outer(c, c) / GROUP_DIM
    chan = np.concatenate([np.cos(ang_c), -np.sin(ang_c)], axis=1) / math.sqrt(GROUP_DIM)
    n1 = np.arange(rows)
    k1 = np.arange(rows)
    mats = []
    for k2 in range(FFT_BLOCKS):
        ang = 2.0 * np.pi * np.outer(FFT_BLOCKS * k1 + k2, n1) / seq
        mats.append(np.concatenate([np.cos(ang), np.sin(ang)], axis=1) / math.sqrt(seq))
    seq_mats = np.stack(mats)
    tile = FFT_BLOCKS * FFT_BLOCKS
    rho = np.arange(tile)
    perm = np.zeros((tile, tile), np.float32)
    perm[rho, FFT_BLOCKS * (rho % FFT_BLOCKS) + rho // FFT_BLOCKS] = 1.0
    return (jnp.asarray(chan, _F32), jnp.asarray(seq_mats, _F32), jnp.asarray(perm, _F32))


def _fft_across_blocks(xs):
    n = len(xs)
    if n == 1:
        return xs
    ev = _fft_across_blocks(xs[0::2])
    od = _fft_across_blocks(xs[1::2])
    out = [None] * n
    for k in range(n // 2):
        o_r, o_i = od[k]
        e_r, e_i = ev[k]
        if k == 0:
            t_r, t_i = o_r, o_i
        elif 4 * k == n:
            out[k] = (e_r + o_i, e_i - o_r)
            out[k + n // 2] = (e_r - o_i, e_i + o_r)
            continue
        else:
            c = math.cos(2.0 * math.pi * k / n)
            s = -math.sin(2.0 * math.pi * k / n)
            t_r = c * o_r - s * o_i
            t_i = c * o_i + s * o_r
        out[k] = (e_r + t_r, e_i + t_i)
        out[k + n // 2] = (e_r - t_r, e_i - t_i)
    return out


def _fourier_kernel(z_ref, chan_ref, seq_ref, perm_ref, o_ref, w_ref, b_ref, g_ref):
    seq = z_ref.shape[1]
    rows = seq // FFT_BLOCKS
    gd = GROUP_DIM

    for blk in range(FFT_BLOCKS):
        sl = pl.ds(blk * rows, rows)
        w_ref[sl, :] = _dot(z_ref[0, sl, :], chan_ref[...])

    def butterfly(r, carry):
        r0 = pl.multiple_of(r * SUBLANES, SUBLANES)
        for lc in range(gd // LANES):
            re_l = pl.ds(lc * LANES, LANES)
            im_l = pl.ds(gd + lc * LANES, LANES)
            xs = [(w_ref[pl.ds(blk * rows + r0, SUBLANES), re_l],
                   w_ref[pl.ds(blk * rows + r0, SUBLANES), im_l]) for blk in range(FFT_BLOCKS)]
            ys = _fft_across_blocks(xs)
            for k2 in range(FFT_BLOCKS):
                b_ref[k2, pl.ds(r0, SUBLANES), pl.ds(lc * LANES, LANES)] = ys[k2][0]
                b_ref[k2, pl.ds(rows + r0, SUBLANES), pl.ds(lc * LANES, LANES)] = ys[k2][1]
        return carry

    lax.fori_loop(0, rows // SUBLANES, butterfly, 0)

    for k2 in range(FFT_BLOCKS):
        yk = _dot(seq_ref[k2], b_ref[k2].astype(_BF16)).astype(_BF16)
        for a in range(rows // FFT_BLOCKS):
            g_ref[a, pl.ds(FFT_BLOCKS * k2, FFT_BLOCKS), :] = yk[FFT_BLOCKS * a:FFT_BLOCKS * (a + 1), :]

    tile = FFT_BLOCKS * FFT_BLOCKS
    for a in range(rows // FFT_BLOCKS):
        o_ref[0, pl.ds(a * tile, tile), :] = _dot(perm_ref[...], g_ref[a]).astype(o_ref.dtype)


def _fourier_call(z_four):
    b, s, w = z_four.shape
    rows = s // FFT_BLOCKS
    assert rows == FFT_BLOCKS * FFT_BLOCKS and w % GROUP_DIM == 0
    chan, seq_mats, perm = (c.astype(_BF16) for c in _fourier_constants(s))
    blk = pl.BlockSpec((1, s, GROUP_DIM), lambda i, g: (i, 0, g))
    return pl.pallas_call(
        _fourier_kernel,
        grid=(b, w // GROUP_DIM),
        in_specs=[
            blk,
            pl.BlockSpec(chan.shape, lambda i, g: (0, 0)),
            pl.BlockSpec(seq_mats.shape, lambda i, g: (0, 0, 0)),
            pl.BlockSpec(perm.shape, lambda i, g: (0, 0)),
        ],
        out_specs=blk,
        out_shape=jax.ShapeDtypeStruct((b, s, w), _BF16),
        scratch_shapes=[
            pltpu.VMEM((s, 2 * GROUP_DIM), _F32),
            pltpu.VMEM((FFT_BLOCKS, 2 * rows, GROUP_DIM), _F32),
            pltpu.VMEM((rows // FFT_BLOCKS, FFT_BLOCKS * FFT_BLOCKS, GROUP_DIM), _BF16),
        ],
        compiler_params=pltpu.CompilerParams(
            dimension_semantics=("parallel", "parallel"), vmem_limit_bytes=VMEM_LIMIT_BYTES),
        name="fourier_mix",
    )(z_four, chan, seq_mats, perm)


def _softplus(x):
    return jnp.maximum(x, 0.0) + jnp.log1p(jnp.exp(-jnp.abs(x)))


def _lru_kernel(zr_ref, zg_ref, cw_ref, cb_ref, wcat_ref, bcat_ref, lam_ref, o_ref,
                zpad_ref, af_ref, uf_ref, ab_ref, ub_ref, hf_ref, hb_ref, tot_ref, cin_ref):
    seq = zr_ref.shape[1]
    hd = HEAD_DIM
    n_chunks = seq // SCAN_CHUNK
    n_groups = n_chunks // SUBLANES
    halo = SUBLANES

    def fill(c, carry):
        r0 = pl.multiple_of(c * SCAN_CHUNK, SCAN_CHUNK)
        p0 = pl.multiple_of(c * CONV_PITCH, SUBLANES)
        zpad_ref[pl.ds(p0 + halo, SCAN_CHUNK), :] = zr_ref[0, pl.ds(r0, SCAN_CHUNK), :].astype(_F32)
        return carry

    lax.fori_loop(0, n_chunks, fill, 0)
    for c in range(n_chunks):
        before = (zpad_ref[pl.ds((c - 1) * CONV_PITCH + SCAN_CHUNK, halo), :] if c > 0
                  else jnp.zeros((halo, hd), _F32))
        after = (zpad_ref[pl.ds((c + 1) * CONV_PITCH + halo, halo), :] if c < n_chunks - 1
                 else jnp.zeros((halo, hd), _F32))
        zpad_ref[pl.ds(c * CONV_PITCH, halo), :] = before
        zpad_ref[pl.ds(c * CONV_PITCH + halo + SCAN_CHUNK, halo), :] = after

    cw = 0.5 * cw_ref[...]
    cb = 0.5 * cb_ref[...]
    half_rate = (-0.5 * LRU_C * LOG2_E) * _softplus(-lam_ref[0])
    directions = ((af_ref, uf_ref, hf_ref), (ab_ref, ub_ref, hb_ref))

    def gates(tb, carry):
        t0 = tb * GATE_STEPS
        taps = [zpad_ref[pl.ds(halo - 2 + t0 + m, n_chunks, stride=CONV_PITCH), :]
                for m in range(GATE_STEPS + 3)]
        hv = jnp.concatenate(
            [cb + sum(taps[j + k] * cw[k:k + 1, :] for k in range(4)) for j in range(GATE_STEPS)], axis=0)
        th = jnp.tanh(_dot(hv.astype(_BF16), wcat_ref[0]) + bcat_ref[0])
        rows = pl.ds(pl.multiple_of(t0 * n_chunks, GATE_STEPS * n_chunks), GATE_STEPS * n_chunks)
        for d, (a_ref, u_ref, _) in enumerate(directions):
            th_r = th[:, (2 * d) * hd:(2 * d + 1) * hd]
            th_i = th[:, (2 * d + 1) * hd:(2 * d + 2) * hd]
            hr = half_rate[d:d + 1, :]
            log2_a = hr + hr * th_r
            a = jnp.exp2(log2_a)
            x = jnp.tanh(log2_a * (-LN_2)) * (1.0 + a * a)
            scale = jnp.where(x > 0.0, x * lax.rsqrt(x), 0.0)
            a_ref[rows, :] = a
            u_ref[rows, :] = scale * (1.0 + th_i) * hv
        return carry

    lax.fori_loop(0, SCAN_CHUNK // GATE_STEPS, gates, 0, unroll=2)

    def step_rows(d, g, i):
        t = i if d == 0 else SCAN_CHUNK - 1 - i
        return pl.ds(pl.multiple_of(t * n_chunks, n_chunks) + g * SUBLANES, SUBLANES)

    def chunk_rows(d, g, i):
        t = i if d == 0 else SCAN_CHUNK - 1 - i
        return pl.ds(g * SUBLANES * SCAN_PITCH + t, SUBLANES, stride=SCAN_PITCH)

    lanes = [(d, g) for d in range(2) for g in range(n_groups)]

    def totals(i, carry):
        out = []
        for (d, g), (prod, resp) in zip(lanes, carry):
            a = directions[d][0][step_rows(d, g, i), :]
            u = directions[d][1][step_rows(d, g, i), :]
            out.append((prod * a, a * resp + u))
        return tuple(out)

    init = tuple((jnp.ones((SUBLANES, hd), _F32), jnp.zeros((SUBLANES, hd), _F32)) for _ in lanes)
    tot = lax.fori_loop(0, SCAN_CHUNK, totals, init, unroll=SCAN_UNROLL)
    for (d, g), (prod, resp) in zip(lanes, tot):
        tot_ref[2 * d, pl.ds(g * SUBLANES, SUBLANES), :] = prod
        tot_ref[2 * d + 1, pl.ds(g * SUBLANES, SUBLANES), :] = resp

    for d in range(2):
        state = jnp.zeros((1, hd), _F32)
        for c in (range(n_chunks) if d == 0 else range(n_chunks - 1, -1, -1)):
            cin_ref[d, pl.ds(c, 1), :] = state
            state = tot_ref[2 * d, pl.ds(c, 1), :] * state + tot_ref[2 * d + 1, pl.ds(c, 1), :]

    def replay(i, carry):
        out = []
        for (d, g), h_in in zip(lanes, carry):
            a = directions[d][0][step_rows(d, g, i), :]
            u = directions[d][1][step_rows(d, g, i), :]
            h = a * h_in + u
            directions[d][2][chunk_rows(d, g, i), :] = h
            out.append(h)
        return tuple(out)

    lax.fori_loop(0, SCAN_CHUNK, replay,
                  tuple(cin_ref[d, pl.ds(g * SUBLANES, SUBLANES), :] for d, g in lanes), unroll=SCAN_UNROLL)

    def finish(c, carry):
        r0 = pl.multiple_of(c * SCAN_CHUNK, SCAN_CHUNK)
        p0 = pl.multiple_of(c * SCAN_PITCH, SUBLANES)
        h = hf_ref[pl.ds(p0, SCAN_CHUNK), :] + hb_ref[pl.ds(p0, SCAN_CHUNK), :]
        zg = zg_ref[0, pl.ds(r0, SCAN_CHUNK), :].astype(_F32)
        o_ref[0, pl.ds(r0, SCAN_CHUNK), :] = (h * _gelu_tanh(zg)).astype(o_ref.dtype)
        return carry

    lax.fori_loop(0, n_chunks, finish, 0)


def _lru_call(z_rec, z_gelu, conv_w, conv_b, wcat, bcat, lam):
    b, s, w = z_rec.shape
    heads = w // HEAD_DIM
    n_chunks = s // SCAN_CHUNK
    assert s % (SCAN_CHUNK * SUBLANES) == 0
    blk = pl.BlockSpec((1, s, HEAD_DIM), lambda i, h: (i, 0, h))
    pitched = pltpu.VMEM((n_chunks * SCAN_PITCH, HEAD_DIM), _F32)
    time_major = pltpu.VMEM((s, HEAD_DIM), _F32)
    return pl.pallas_call(
        _lru_kernel,
        grid=(b, heads),
        in_specs=[
            blk,
            blk,
            pl.BlockSpec((conv_w.shape[0], HEAD_DIM), lambda i, h: (0, h)),
            pl.BlockSpec((1, HEAD_DIM), lambda i, h: (0, h)),
            pl.BlockSpec((1, HEAD_DIM, 4 * HEAD_DIM), lambda i, h: (h, 0, 0)),
            pl.BlockSpec((1, 1, 4 * HEAD_DIM), lambda i, h: (h, 0, 0)),
            pl.BlockSpec((1, 2, HEAD_DIM), lambda i, h: (h, 0, 0)),
        ],
        out_specs=blk,
        out_shape=jax.ShapeDtypeStruct((b, s, w), _BF16),
        scratch_shapes=[
            pltpu.VMEM((n_chunks * CONV_PITCH, HEAD_DIM), _F32),
            time_major, time_major, time_major, time_major, pitched, pitched,
            pltpu.VMEM((4, n_chunks, HEAD_DIM), _F32),
            pltpu.VMEM((2, n_chunks, HEAD_DIM), _F32),
        ],
        compiler_params=pltpu.CompilerParams(
            dimension_semantics=("parallel", "parallel"), vmem_limit_bytes=VMEM_LIMIT_BYTES),
        name="rglru",
    )(z_rec, z_gelu, conv_w, conv_b, wcat, bcat, lam)


def _merge_kernel(x_ref, yf_ref, yg_ref, gates_ref, pa_ref, pb_ref, wo_ref, o_ref):
    d = x_ref.shape[1]
    y_a = _dot(yf_ref[...], pa_ref[...])
    y_b = _dot(yg_ref[...], pb_ref[...])
    merged = gates_ref[:, :d].astype(_F32) * y_a + gates_ref[:, d:].astype(_F32) * y_b
    o_ref[...] = x_ref[...] + _dot(merged.astype(_BF16), wo_ref[...])


def _merge_call(x, y_four, y_rec, gates, proj_a, proj_b, w_out, *, tm=256):
    t, d = x.shape
    resident = functools.partial(pl.BlockSpec, index_map=lambda i: (0, 0), pipeline_mode=pl.Buffered(1))
    return pl.pallas_call(
        _merge_kernel,
        grid=(t // tm,),
        in_specs=[
            pl.BlockSpec((tm, d), lambda i: (i, 0)),
            pl.BlockSpec((tm, y_four.shape[1]), lambda i: (i, 0)),
            pl.BlockSpec((tm, y_rec.shape[1]), lambda i: (i, 0)),
            pl.BlockSpec((tm, gates.shape[1]), lambda i: (i, 0)),
            resident(proj_a.shape),
            resident(proj_b.shape),
            resident(w_out.shape),
        ],
        out_specs=pl.BlockSpec((tm, d), lambda i: (i, 0)),
        out_shape=jax.ShapeDtypeStruct((t, d), _F32),
        compiler_params=pltpu.CompilerParams(
            dimension_semantics=("parallel",), vmem_limit_bytes=VMEM_LIMIT_BYTES),
        name="merge",
    )(x, y_four, y_rec, gates, proj_a, proj_b, w_out)


def kernel(x, ffn1_norm, ffn1_w_gate, ffn1_w_up, ffn1_w_down, mix_norm, w_in, b_gates, conv_w, conv_b, lru_wa, lru_ba, lru_wx, lru_bx, lru_lambda, proj_a, proj_b, w_out, ffn2_norm, ffn2_w_gate, ffn2_w_up, ffn2_w_down, final_norm):
    b, s, d = x.shape
    assert ffn1_norm.shape[0] == 1, "single-layer problem"
    bf = lambda w: w.astype(_BF16)
    x1, u, w_in_b = _ffn_call(
        x.reshape(b * s, d), ffn1_norm, bf(ffn1_w_gate[0]), bf(ffn1_w_up[0]), bf(ffn1_w_down[0]),
        mix_norm, emit_residual=True, casts=(w_in[0],))
    z_four, z_rec, z_gelu, gates, wg2, wu2, wd2, proj_a_b, proj_b_b, w_out_b = _inproj_call(
        u, w_in_b, b_gates.reshape(1, GATE_WIDTH),
        casts=(ffn2_w_gate[0], ffn2_w_up[0], ffn2_w_down[0], proj_a[0], proj_b[0], w_out[0]))

    y_four = _fourier_call(z_four.reshape(b, s, FOURIER_WIDTH))

    wcat = jnp.concatenate([lru_wa[0, 0], lru_wx[0, 0], lru_wa[0, 1], lru_wx[0, 1]], axis=-1)
    bcat = 0.5 * jnp.concatenate([lru_ba[0, 0], lru_bx[0, 0], lru_ba[0, 1], lru_bx[0, 1]], axis=-1)
    lam = jnp.transpose(lru_lambda[0], (1, 0, 2))
    y_rec = _lru_call(z_rec.reshape(b, s, LRU_WIDTH), z_gelu.reshape(b, s, LRU_WIDTH),
                      conv_w[0], conv_b, bf(wcat), bcat[:, None, :], lam)

    x2 = _merge_call(x1, y_four.reshape(b * s, FOURIER_WIDTH), y_rec.reshape(b * s, LRU_WIDTH), gates,
                     proj_a_b, proj_b_b, w_out_b)
    (out,) = _ffn_call(x2, ffn2_norm, wg2, wu2, wd2, final_norm[None], emit_residual=False)
    return out.reshape(b, s, d)
```

```python
import functools
import math

import numpy as np
import jax
import jax.numpy as jnp
from jax import lax
from jax.experimental import pallas as pl
from jax.experimental.pallas import tpu as pltpu

D_MODEL = 2048
D_FF = 5632
FOURIER_WIDTH = 1024
FOURIER_GROUPS = 4
GROUP_DIM = FOURIER_WIDTH // FOURIER_GROUPS
LRU_WIDTH = 1024
LRU_HEADS = 8
HEAD_DIM = LRU_WIDTH // LRU_HEADS
LRU_C = 8.0
RMS_EPS = 1e-6
FFN_RES_SCALE = 0.5
GATE_WIDTH = 2 * D_MODEL
IN_WIDTH = FOURIER_WIDTH + 2 * LRU_WIDTH + GATE_WIDTH

VMEM_LIMIT_BYTES = 56 * 1024 * 1024
SUBLANES = 8
LANES = 128
CAST_ROWS = 2 * SUBLANES

FFT_BLOCKS = 16
SCAN_CHUNK = 128
SCAN_PITCH = SCAN_CHUNK + SUBLANES
CONV_PITCH = SCAN_CHUNK + 3 * SUBLANES
PIECE_ROWS = 64
GATE_STEPS = 8
GATE_PIECE = 256
SCAN_UNROLL = 4
LOG2_E = 1.0 / math.log(2.0)
LN_2 = math.log(2.0)

_BF16 = jnp.bfloat16
_F32 = jnp.float32


def _dot(a, b):
    return jnp.dot(a, b, preferred_element_type=_F32)


def _sigmoid(x):
    return 0.5 * (1.0 + jnp.tanh(0.5 * x))


def _rms_norm(x, gain):
    ms = jnp.mean(x * x, axis=-1, keepdims=True)
    return x * lax.rsqrt(ms + RMS_EPS) * gain


def _gelu_tanh(x):
    return 0.5 * x * (1.0 + jnp.tanh(math.sqrt(2.0 / math.pi) * (x + 0.044715 * (x * x * x))))


def _ride_along_casts(weights, grid):
    n_steps = grid[0] * grid[1]
    specs, shapes = [], []
    for w in weights:
        rows, cols = w.shape
        block_rows = next(r for r in range(CAST_ROWS, rows + 1, CAST_ROWS)
                          if rows % r == 0 and rows // r <= n_steps)
        n_blocks = rows // block_rows
        specs.append(pl.BlockSpec(
            (block_rows, cols),
            lambda i, j, n_blocks=n_blocks: ((i * grid[1] + j) * n_blocks // n_steps, 0)))
        shapes.append(jax.ShapeDtypeStruct((rows, cols), _BF16))
    return specs, shapes


def _cast_blocks(srcs, dsts):
    for src, dst in zip(srcs, dsts):
        dst[...] = src[...].astype(_BF16)


def _ffn_kernel(*refs, emit_residual, n_casts):
    x_ref, gain_ref, wg_ref, wu_ref, wd_ref, post_ref = refs[:6]
    cast_in = refs[6:6 + n_casts]
    outs = refs[6 + n_casts:]
    if emit_residual:
        res_ref, normed_ref = outs[:2]
        outs = outs[2:]
    else:
        normed_ref = outs[0]
        outs = outs[1:]
    cast_out = outs[:n_casts]
    h_ref, acc_ref = outs[n_casts:]
    j = pl.program_id(1)

    @pl.when(j == 0)
    def _():
        h_ref[...] = _rms_norm(x_ref[...], gain_ref[...]).astype(_BF16)
        acc_ref[...] = jnp.zeros_like(acc_ref)

    h = h_ref[...]
    g = _dot(h, wg_ref[...])
    u = _dot(h, wu_ref[...])
    act = (g * _sigmoid(g) * u).astype(_BF16)
    acc_ref[...] += _dot(act, wd_ref[...])

    _cast_blocks(cast_in, cast_out)

    @pl.when(j == pl.num_programs(1) - 1)
    def _():
        y = x_ref[...] + FFN_RES_SCALE * acc_ref[...]
        if emit_residual:
            res_ref[...] = y
        normed_ref[...] = _rms_norm(y, post_ref[...]).astype(normed_ref.dtype)


def _ffn_call(x, gain, wg, wu, wd, post_gain, *, emit_residual, casts=(), tm=512, tf=512):
    t, d = x.shape
    f = wg.shape[1]
    grid = (t // tm, f // tf)
    row_spec = pl.BlockSpec((tm, d), lambda i, j: (i, 0))
    vec_spec = pl.BlockSpec((1, d), lambda i, j: (0, 0))
    in_specs = [
        row_spec,
        vec_spec,
        pl.BlockSpec((d, tf), lambda i, j: (0, j)),
        pl.BlockSpec((d, tf), lambda i, j: (0, j)),
        pl.BlockSpec((tf, d), lambda i, j: (j, 0)),
        vec_spec,
    ]
    out_shape = [jax.ShapeDtypeStruct((t, d), _BF16 if emit_residual else _F32)]
    out_specs = [row_spec]
    if emit_residual:
        out_shape.insert(0, jax.ShapeDtypeStruct((t, d), _F32))
        out_specs.insert(0, row_spec)
    cast_specs, cast_shapes = _ride_along_casts(casts, grid)
    in_specs += cast_specs
    out_specs += cast_specs
    out_shape += cast_shapes
    return pl.pallas_call(
        functools.partial(_ffn_kernel, emit_residual=emit_residual, n_casts=len(casts)),
        grid=grid,
        in_specs=in_specs,
        out_specs=tuple(out_specs),
        out_shape=tuple(out_shape),
        scratch_shapes=[pltpu.VMEM((tm, d), _BF16), pltpu.VMEM((tm, d), _F32)],
        compiler_params=pltpu.CompilerParams(
            dimension_semantics=("arbitrary", "arbitrary"), vmem_limit_bytes=VMEM_LIMIT_BYTES),
        name="ffn_residual" if emit_residual else "ffn_final",
    )(x, gain, wg, wu, wd, post_gain, *casts)


def _inproj_kernel(*refs, n_casts):
    u_ref, w_ref = refs[:2]
    cast_in = refs[2:2 + n_casts]
    mixer_refs = refs[2 + n_casts:5 + n_casts]
    cast_out = refs[5 + n_casts:]
    j = pl.program_id(1)
    for col, ref in enumerate(mixer_refs):
        @pl.when(j == col)
        def _(ref=ref):
            ref[...] = _dot(u_ref[...], w_ref[...]).astype(_BF16)
            _cast_blocks(cast_in, cast_out)


def _inproj_call(u, w_in, *, casts=(), tm=1024, tn=1024):
    t, d = u.shape
    assert FOURIER_WIDTH == LRU_WIDTH == tn
    grid = (t // tm, 3)
    mixer_spec = pl.BlockSpec((tm, tn), lambda i, j: (i, 0))
    cast_specs, cast_shapes = _ride_along_casts(casts, grid)
    return pl.pallas_call(
        functools.partial(_inproj_kernel, n_casts=len(casts)),
        grid=grid,
        in_specs=[
            pl.BlockSpec((tm, d), lambda i, j: (i, 0)),
            pl.BlockSpec((d, tn), lambda i, j: (0, j)),
        ] + cast_specs,
        out_specs=(mixer_spec, mixer_spec, mixer_spec, *cast_specs),
        out_shape=(
            jax.ShapeDtypeStruct((t, FOURIER_WIDTH), _BF16),
            jax.ShapeDtypeStruct((t, LRU_WIDTH), _BF16),
            jax.ShapeDtypeStruct((t, LRU_WIDTH), _BF16),
            *cast_shapes,
        ),
        compiler_params=pltpu.CompilerParams(
            dimension_semantics=("arbitrary", "arbitrary"), vmem_limit_bytes=VMEM_LIMIT_BYTES),
        name="inproj",
    )(u, w_in, *casts)


def _fourier_constants(seq):
    rows = seq // FFT_BLOCKS
    c = np.arange(GROUP_DIM)
    ang_c = 2.0 * np.pi * np.outer(c, c) / GROUP_DIM
    chan = np.concatenate([np.cos(ang_c), -np.sin(ang_c)], axis=1) / math.sqrt(GROUP_DIM)
    n1 = np.arange(rows)
    k1 = np.arange(rows)
    mats = []
    for k2 in range(FFT_BLOCKS):
        ang = 2.0 * np.pi * np.outer(FFT_BLOCKS * k1 + k2, n1) / seq
        mats.append(np.concatenate([np.cos(ang), np.sin(ang)], axis=1) / math.sqrt(seq))
    seq_mats = np.stack(mats)
    tile = FFT_BLOCKS * FFT_BLOCKS
    rho = np.arange(tile)
    perm = np.zeros((tile, tile), np.float32)
    perm[rho, FFT_BLOCKS * (rho % FFT_BLOCKS) + rho // FFT_BLOCKS] = 1.0
    return (jnp.asarray(chan, _F32), jnp.asarray(seq_mats, _F32), jnp.asarray(perm, _F32))


def _fft_across_blocks(xs):
    n = len(xs)
    if n == 1:
        return xs
    ev = _fft_across_blocks(xs[0::2])
    od = _fft_across_blocks(xs[1::2])
    out = [None] * n
    for k in range(n // 2):
        o_r, o_i = od[k]
        e_r, e_i = ev[k]
        if k == 0:
            t_r, t_i = o_r, o_i
        elif 4 * k == n:
            out[k] = (e_r + o_i, e_i - o_r)
            out[k + n // 2] = (e_r - o_i, e_i + o_r)
            continue
        else:
            c = math.cos(2.0 * math.pi * k / n)
            s = -math.sin(2.0 * math.pi * k / n)
            t_r = c * o_r - s * o_i
            t_i = c * o_i + s * o_r
        out[k] = (e_r + t_r, e_i + t_i)
        out[k + n // 2] = (e_r - t_r, e_i - t_i)
    return out


def _fourier_kernel(z_ref, chan_ref, seq_ref, perm_ref, o_ref, w_ref, b_ref, g_ref):
    seq = z_ref.shape[1]
    rows = seq // FFT_BLOCKS
    gd = GROUP_DIM

    for blk in range(FFT_BLOCKS):
        sl = pl.ds(blk * rows, rows)
        w_ref[sl, :] = _dot(z_ref[0, sl, :], chan_ref[...])

    def butterfly(r, carry):
        r0 = pl.multiple_of(r * SUBLANES, SUBLANES)
        for lc in range(gd // LANES):
            re_l = pl.ds(lc * LANES, LANES)
            im_l = pl.ds(gd + lc * LANES, LANES)
            xs = [(w_ref[pl.ds(blk * rows + r0, SUBLANES), re_l],
                   w_ref[pl.ds(blk * rows + r0, SUBLANES), im_l]) for blk in range(FFT_BLOCKS)]
            ys = _fft_across_blocks(xs)
            for k2 in range(FFT_BLOCKS):
                b_ref[k2, pl.ds(r0, SUBLANES), pl.ds(lc * LANES, LANES)] = ys[k2][0]
                b_ref[k2, pl.ds(rows + r0, SUBLANES), pl.ds(lc * LANES, LANES)] = ys[k2][1]
        return carry

    lax.fori_loop(0, rows // SUBLANES, butterfly, 0)

    for k2 in range(FFT_BLOCKS):
        yk = _dot(seq_ref[k2], b_ref[k2].astype(_BF16)).astype(_BF16)
        for a in range(rows // FFT_BLOCKS):
            g_ref[a, pl.ds(FFT_BLOCKS * k2, FFT_BLOCKS), :] = yk[FFT_BLOCKS * a:FFT_BLOCKS * (a + 1), :]

    tile = FFT_BLOCKS * FFT_BLOCKS
    for a in range(rows // FFT_BLOCKS):
        o_ref[0, pl.ds(a * tile, tile), :] = _dot(perm_ref[...], g_ref[a]).astype(o_ref.dtype)


def _fourier_call(z_four):
    b, s, w = z_four.shape
    rows = s // FFT_BLOCKS
    assert rows == FFT_BLOCKS * FFT_BLOCKS and w % GROUP_DIM == 0
    chan, seq_mats, perm = (c.astype(_BF16) for c in _fourier_constants(s))
    blk = pl.BlockSpec((1, s, GROUP_DIM), lambda i, g: (i, 0, g))
    return pl.pallas_call(
        _fourier_kernel,
        grid=(b, w // GROUP_DIM),
        in_specs=[
            blk,
            pl.BlockSpec(chan.shape, lambda i, g: (0, 0)),
            pl.BlockSpec(seq_mats.shape, lambda i, g: (0, 0, 0)),
            pl.BlockSpec(perm.shape, lambda i, g: (0, 0)),
        ],
        out_specs=blk,
        out_shape=jax.ShapeDtypeStruct((b, s, w), _BF16),
        scratch_shapes=[
            pltpu.VMEM((s, 2 * GROUP_DIM), _F32),
            pltpu.VMEM((FFT_BLOCKS, 2 * rows, GROUP_DIM), _F32),
            pltpu.VMEM((rows // FFT_BLOCKS, FFT_BLOCKS * FFT_BLOCKS, GROUP_DIM), _BF16),
        ],
        compiler_params=pltpu.CompilerParams(
            dimension_semantics=("parallel", "parallel"), vmem_limit_bytes=VMEM_LIMIT_BYTES),
        name="fourier_mix",
    )(z_four, chan, seq_mats, perm)


def _softplus(x):
    return jnp.maximum(x, 0.0) + jnp.log1p(jnp.exp(-jnp.abs(x)))


def _lru_kernel(zr_ref, zg_ref, cw_ref, cb_ref, wcat_ref, bcat_ref, lam_ref,
                tok_ref, gw0_ref, gw1_ref, gw2_ref, gw3_ref, gbias_ref, o_ref, gates_ref,
                zpad_ref, af_ref, uf_ref, ab_ref, ub_ref, hf_ref, hb_ref, tot_ref, cin_ref):
    seq = zr_ref.shape[1]
    hd = HEAD_DIM
    n_chunks = seq // SCAN_CHUNK
    n_groups = n_chunks // SUBLANES
    halo = SUBLANES

    def fill(c, carry):
        r0 = pl.multiple_of(c * SCAN_CHUNK, SCAN_CHUNK)
        p0 = pl.multiple_of(c * CONV_PITCH, SUBLANES)
        zpad_ref[pl.ds(p0 + halo, SCAN_CHUNK), :] = zr_ref[0, pl.ds(r0, SCAN_CHUNK), :].astype(_F32)
        return carry

    lax.fori_loop(0, n_chunks, fill, 0)
    for c in range(n_chunks):
        before = (zpad_ref[pl.ds((c - 1) * CONV_PITCH + SCAN_CHUNK, halo), :] if c > 0
                  else jnp.zeros((halo, hd), _F32))
        after = (zpad_ref[pl.ds((c + 1) * CONV_PITCH + halo, halo), :] if c < n_chunks - 1
                 else jnp.zeros((halo, hd), _F32))
        zpad_ref[pl.ds(c * CONV_PITCH, halo), :] = before
        zpad_ref[pl.ds(c * CONV_PITCH + halo + SCAN_CHUNK, halo), :] = after

    cw = 0.5 * cw_ref[...]
    cb = 0.5 * cb_ref[...]
    half_rate = (-0.5 * LRU_C * LOG2_E) * _softplus(-lam_ref[0])
    directions = ((af_ref, uf_ref, hf_ref), (ab_ref, ub_ref, hb_ref))

    def gates(tb):
        t0 = tb * GATE_STEPS
        taps = [zpad_ref[pl.ds(halo - 2 + t0 + m, n_chunks, stride=CONV_PITCH), :]
                for m in range(GATE_STEPS + 3)]
        hvs = [cb + sum(taps[j + k] * cw[k:k + 1, :] for k in range(4)) for j in range(GATE_STEPS)]
        g_half = _dot(jnp.concatenate(hvs, axis=0).astype(_BF16), wcat_ref[0])
        for j, hv in enumerate(hvs):
            rows = pl.ds((t0 + j) * n_chunks, n_chunks)
            th = jnp.tanh(g_half[j * n_chunks:(j + 1) * n_chunks, :] + bcat_ref[0])
            for d, (a_ref, u_ref, _) in enumerate(directions):
                th_r = th[:, (2 * d) * hd:(2 * d + 1) * hd]
                th_i = th[:, (2 * d + 1) * hd:(2 * d + 2) * hd]
                hr = half_rate[d:d + 1, :]
                log2_a = hr + hr * th_r
                a = jnp.exp2(log2_a)
                x = jnp.tanh(log2_a * (-LN_2)) * (1.0 + a * a)
                scale = jnp.where(x > 0.0, x * lax.rsqrt(x), 0.0)
                a_ref[rows, :] = a
                u_ref[rows, :] = scale * (1.0 + th_i) * hv

    def gate_piece(n):
        w_ref = (gw0_ref, gw1_ref, gw2_ref, gw3_ref)[n // pieces_per_ref]
        c0 = (n % pieces_per_ref) * GATE_PIECE
        cols = pl.ds(n * GATE_PIECE, GATE_PIECE)
        z = _dot(tok_ref[...], w_ref[:, c0:c0 + GATE_PIECE])
        for r in range(0, z.shape[0], PIECE_ROWS):
            gates_ref[pl.ds(r, PIECE_ROWS), cols] = _sigmoid(
                z[r:r + PIECE_ROWS, :] + gbias_ref[:, cols]).astype(gates_ref.dtype)

    n_pieces = gates_ref.shape[1] // GATE_PIECE
    pieces_per_ref = gw0_ref.shape[1] // GATE_PIECE
    n_blocks = SCAN_CHUNK // GATE_STEPS
    assert n_blocks % n_pieces == 0
    for tb in range(n_blocks):
        gates(tb)
        if (tb + 1) % (n_blocks // n_pieces) == 0:
            gate_piece(tb // (n_blocks // n_pieces))

    def step_rows(d, g, i):
        t = i if d == 0 else SCAN_CHUNK - 1 - i
        return pl.ds(pl.multiple_of(t * n_chunks, n_chunks) + g * SUBLANES, SUBLANES)

    def chunk_rows(d, g, i):
        t = i if d == 0 else SCAN_CHUNK - 1 - i
        return pl.ds(g * SUBLANES * SCAN_PITCH + t, SUBLANES, stride=SCAN_PITCH)

    lanes = [(d, g) for d in range(2) for g in range(n_groups)]

    def totals(i, carry):
        out = []
        for (d, g), (prod, resp) in zip(lanes, carry):
            a = directions[d][0][step_rows(d, g, i), :]
            u = directions[d][1][step_rows(d, g, i), :]
            out.append((prod * a, a * resp + u))
        return tuple(out)

    init = tuple((jnp.ones((SUBLANES, hd), _F32), jnp.zeros((SUBLANES, hd), _F32)) for _ in lanes)
    tot = lax.fori_loop(0, SCAN_CHUNK, totals, init, unroll=SCAN_UNROLL)
    for (d, g), (prod, resp) in zip(lanes, tot):
        tot_ref[2 * d, pl.ds(g * SUBLANES, SUBLANES), :] = prod
        tot_ref[2 * d + 1, pl.ds(g * SUBLANES, SUBLANES), :] = resp

    for d in range(2):
        state = jnp.zeros((1, hd), _F32)
        for c in (range(n_chunks) if d == 0 else range(n_chunks - 1, -1, -1)):
            cin_ref[d, pl.ds(c, 1), :] = state
            state = tot_ref[2 * d, pl.ds(c, 1), :] * state + tot_ref[2 * d + 1, pl.ds(c, 1), :]

    def replay(i, carry):
        out = []
        for (d, g), h_in in zip(lanes, carry):
            a = directions[d][0][step_rows(d, g, i), :]
            u = directions[d][1][step_rows(d, g, i), :]
            h = a * h_in + u
            directions[d][2][chunk_rows(d, g, i), :] = h
            out.append(h)
        return tuple(out)

    lax.fori_loop(0, SCAN_CHUNK, replay,
                  tuple(cin_ref[d, pl.ds(g * SUBLANES, SUBLANES), :] for d, g in lanes), unroll=SCAN_UNROLL)

    def finish(c, carry):
        r0 = pl.multiple_of(c * SCAN_CHUNK, SCAN_CHUNK)
        p0 = pl.multiple_of(c * SCAN_PITCH, SUBLANES)
        h = hf_ref[pl.ds(p0, SCAN_CHUNK), :] + hb_ref[pl.ds(p0, SCAN_CHUNK), :]
        zg = zg_ref[0, pl.ds(r0, SCAN_CHUNK), :].astype(_F32)
        o_ref[0, pl.ds(r0, SCAN_CHUNK), :] = (h * _gelu_tanh(zg)).astype(o_ref.dtype)
        return carry

    lax.fori_loop(0, n_chunks, finish, 0)


def _lru_call(z_rec, z_gelu, conv_w, conv_b, wcat, bcat, lam, tokens, w_in, b_gates):
    b, s, w = z_rec.shape
    heads = w // HEAD_DIM
    n_chunks = s // SCAN_CHUNK
    assert s % (SCAN_CHUNK * SUBLANES) == 0
    t, d = tokens.shape
    tile = t // (b * heads)
    n_wrefs = 4
    wcols = GATE_WIDTH // n_wrefs
    first = (IN_WIDTH - GATE_WIDTH) // wcols
    assert (IN_WIDTH - GATE_WIDTH) % wcols == 0 and t % (b * heads) == 0
    blk = pl.BlockSpec((1, s, HEAD_DIM), lambda i, h: (i, 0, h))
    pitched = pltpu.VMEM((n_chunks * SCAN_PITCH, HEAD_DIM), _F32)
    time_major = pltpu.VMEM((s, HEAD_DIM), _F32)
    gate_w_specs = [
        pl.BlockSpec((d, wcols), lambda i, h, c=first + k: (0, c), pipeline_mode=pl.Buffered(1))
        for k in range(n_wrefs)]
    return pl.pallas_call(
        _lru_kernel,
        grid=(b, heads),
        in_specs=[
            blk,
            blk,
            pl.BlockSpec((conv_w.shape[0], HEAD_DIM), lambda i, h: (0, h)),
            pl.BlockSpec((1, HEAD_DIM), lambda i, h: (0, h)),
            pl.BlockSpec((1, HEAD_DIM, 4 * HEAD_DIM), lambda i, h: (h, 0, 0)),
            pl.BlockSpec((1, 1, 4 * HEAD_DIM), lambda i, h: (h, 0, 0)),
            pl.BlockSpec((1, 2, HEAD_DIM), lambda i, h: (h, 0, 0)),
            pl.BlockSpec((tile, d), lambda i, h: (i * heads + h, 0)),
            *gate_w_specs,
            pl.BlockSpec((1, GATE_WIDTH), lambda i, h: (0, 0)),
        ],
        out_specs=(blk, pl.BlockSpec((tile, GATE_WIDTH), lambda i, h: (i * heads + h, 0))),
        out_shape=(jax.ShapeDtypeStruct((b, s, w), _BF16), jax.ShapeDtypeStruct((t, GATE_WIDTH), _BF16)),
        scratch_shapes=[
            pltpu.VMEM((n_chunks * CONV_PITCH, HEAD_DIM), _F32),
            time_major, time_major, time_major, time_major, pitched, pitched,
            pltpu.VMEM((4, n_chunks, HEAD_DIM), _F32),
            pltpu.VMEM((2, n_chunks, HEAD_DIM), _F32),
        ],
        compiler_params=pltpu.CompilerParams(
            dimension_semantics=("parallel", "parallel"), vmem_limit_bytes=VMEM_LIMIT_BYTES),
        name="rglru_gates",
    )(z_rec, z_gelu, conv_w, conv_b, wcat, bcat, lam, tokens, w_in, w_in, w_in, w_in, b_gates)


def _merge_kernel(x_ref, yf_ref, yg_ref, gates_ref, pa_ref, pb_ref, wo_ref, o_ref):
    d = x_ref.shape[1]
    y_a = _dot(yf_ref[...], pa_ref[...])
    y_b = _dot(yg_ref[...], pb_ref[...])
    merged = gates_ref[:, :d].astype(_F32) * y_a + gates_ref[:, d:].astype(_F32) * y_b
    o_ref[...] = x_ref[...] + _dot(merged.astype(_BF16), wo_ref[...])


def _merge_call(x, y_four, y_rec, gates, proj_a, proj_b, w_out, *, tm=256):
    t, d = x.shape
    resident = functools.partial(pl.BlockSpec, index_map=lambda i: (0, 0), pipeline_mode=pl.Buffered(1))
    return pl.pallas_call(
        _merge_kernel,
        grid=(t // tm,),
        in_specs=[
            pl.BlockSpec((tm, d), lambda i: (i, 0)),
            pl.BlockSpec((tm, y_four.shape[1]), lambda i: (i, 0)),
            pl.BlockSpec((tm, y_rec.shape[1]), lambda i: (i, 0)),
            pl.BlockSpec((tm, gates.shape[1]), lambda i: (i, 0)),
            resident(proj_a.shape),
            resident(proj_b.shape),
            resident(w_out.shape),
        ],
        out_specs=pl.BlockSpec((tm, d), lambda i: (i, 0)),
        out_shape=jax.ShapeDtypeStruct((t, d), _F32),
        compiler_params=pltpu.CompilerParams(
            dimension_semantics=("parallel",), vmem_limit_bytes=VMEM_LIMIT_BYTES),
        name="merge",
    )(x, y_four, y_rec, gates, proj_a, proj_b, w_out)


def kernel(x, ffn1_norm, ffn1_w_gate, ffn1_w_up, ffn1_w_down, mix_norm, w_in, b_gates, conv_w, conv_b, lru_wa, lru_ba, lru_wx, lru_bx, lru_lambda, proj_a, proj_b, w_out, ffn2_norm, ffn2_w_gate, ffn2_w_up, ffn2_w_down, final_norm):
    b, s, d = x.shape
    assert ffn1_norm.shape[0] == 1, "single-layer problem"
    bf = lambda w: w.astype(_BF16)
    x1, u, w_in_b = _ffn_call(
        x.reshape(b * s, d), ffn1_norm, bf(ffn1_w_gate[0]), bf(ffn1_w_up[0]), bf(ffn1_w_down[0]),
        mix_norm, emit_residual=True, casts=(w_in[0],))
    z_four, z_rec, z_gelu, wg2, wu2, wd2, proj_a_b, proj_b_b, w_out_b = _inproj_call(
        u, w_in_b, casts=(ffn2_w_gate[0], ffn2_w_up[0], ffn2_w_down[0], proj_a[0], proj_b[0], w_out[0]))

    y_four = _fourier_call(z_four.reshape(b, s, FOURIER_WIDTH))

    wcat = jnp.concatenate([lru_wa[0, 0], lru_wx[0, 0], lru_wa[0, 1], lru_wx[0, 1]], axis=-1)
    bcat = 0.5 * jnp.concatenate([lru_ba[0, 0], lru_bx[0, 0], lru_ba[0, 1], lru_bx[0, 1]], axis=-1)
    lam = jnp.transpose(lru_lambda[0], (1, 0, 2))
    y_rec, gates = _lru_call(z_rec.reshape(b, s, LRU_WIDTH), z_gelu.reshape(b, s, LRU_WIDTH),
                             conv_w[0], conv_b, bf(wcat), bcat[:, None, :], lam,
                             u, w_in_b, b_gates.reshape(1, GATE_WIDTH))

    x2 = _merge_call(x1, y_four.reshape(b * s, FOURIER_WIDTH), y_rec.reshape(b * s, LRU_WIDTH), gates,
                     proj_a_b, proj_b_b, w_out_b)
    (out,) = _ffn_call(x2, ffn2_norm, wg2, wu2, wd2, final_norm[None], emit_residual=False)
    return out.reshape(b, s, d)
```

```python
import functools
import math

import numpy as np
import jax
import jax.numpy as jnp
from jax import lax
from jax.experimental import pallas as pl
from jax.experimental.pallas import tpu as pltpu

D_MODEL = 2048
D_FF = 5632
FOURIER_WIDTH = 1024
FOURIER_GROUPS = 4
GROUP_DIM = FOURIER_WIDTH // FOURIER_GROUPS
LRU_WIDTH = 1024
LRU_HEADS = 8
HEAD_DIM = LRU_WIDTH // LRU_HEADS
LRU_C = 8.0
RMS_EPS = 1e-6
FFN_RES_SCALE = 0.5
GATE_WIDTH = 2 * D_MODEL
IN_WIDTH = FOURIER_WIDTH + 2 * LRU_WIDTH + GATE_WIDTH

VMEM_LIMIT_BYTES = 56 * 1024 * 1024
SUBLANES = 8
LANES = 128
CAST_ROWS = 2 * SUBLANES
NORM_CHUNKS = 8

FFT_BLOCKS = 16
SCAN_CHUNK = 128
SCAN_PITCH = SCAN_CHUNK + SUBLANES
CONV_PITCH = SCAN_CHUNK + 3 * SUBLANES
PIECE_ROWS = 64
GATE_STEPS = 8
GATE_PIECE = 256
SCAN_UNROLL = 4
LOG2_E = 1.0 / math.log(2.0)
LN_2 = math.log(2.0)

_BF16 = jnp.bfloat16
_F32 = jnp.float32


def _dot(a, b):
    return jnp.dot(a, b, preferred_element_type=_F32)


def _sigmoid(x):
    return 0.5 * (1.0 + jnp.tanh(0.5 * x))


def _rms_norm(x, gain):
    ms = jnp.mean(x * x, axis=-1, keepdims=True)
    return x * lax.rsqrt(ms + RMS_EPS) * gain


def _gelu_tanh(x):
    return 0.5 * x * (1.0 + jnp.tanh(math.sqrt(2.0 / math.pi) * (x + 0.044715 * (x * x * x))))


def _ride_along_casts(weights, grid, first_row=0, n_rows=None):
    n_rows = grid[0] if n_rows is None else n_rows
    n_steps = n_rows * grid[1]
    specs, shapes = [], []
    for w in weights:
        rows, cols = w.shape
        block_rows = next(r for r in range(CAST_ROWS, rows + 1, CAST_ROWS)
                          if rows % r == 0 and rows // r <= n_steps)
        n_blocks = rows // block_rows

        def index_map(i, j, n_blocks=n_blocks):
            step = jnp.clip((i - first_row) * grid[1] + j, 0, n_steps - 1)
            return (step * n_blocks // n_steps, 0)

        specs.append(pl.BlockSpec((block_rows, cols), index_map))
        shapes.append(jax.ShapeDtypeStruct((rows, cols), _BF16))
    return specs, shapes


def _cast_blocks(srcs, dsts):
    for src, dst in zip(srcs, dsts):
        dst[...] = src[...].astype(_BF16)


def _ffn_kernel(*refs, emit_residual, n_casts, n_tiles):
    x_ref, gain_ref, wg_ref, wu_ref, wd_ref, post_ref = refs[:6]
    cast_in = refs[6:6 + n_casts]
    outs = refs[6 + n_casts:]
    if emit_residual:
        res_ref, normed_ref = outs[:2]
        outs = outs[2:]
    else:
        normed_ref = outs[0]
        outs = outs[1:]
    cast_out = outs[:n_casts]
    h_refs, acc_refs = outs[n_casts:n_casts + 2], outs[n_casts + 2:]
    r = pl.program_id(0)
    j = pl.program_id(1)
    tm = x_ref.shape[0]
    chunk_rows = tm // NORM_CHUNKS
    chunk = pl.ds(pl.multiple_of(jnp.minimum(j, NORM_CHUNKS - 1) * chunk_rows, chunk_rows), chunk_rows)

    @pl.when(jnp.logical_and(r == 0, j == 0))
    def _():
        for acc_ref in acc_refs:
            acc_ref[...] = jnp.zeros_like(acc_ref)

    def norms(slot):
        y = FFN_RES_SCALE * acc_refs[slot][chunk, :]
        if emit_residual:
            res_ref[chunk, :] = y
        post = _rms_norm(y, post_ref[...])
        normed_ref[chunk, :] = post.astype(normed_ref.dtype)
        x = x_ref[chunk, :]
        pre = _rms_norm(x, gain_ref[...])
        h_refs[slot][chunk, :] = pre.astype(_BF16)
        acc_refs[slot][chunk, :] = (1.0 / FFN_RES_SCALE) * x
        both = post + pre
        tiles = [both[i:i + SUBLANES, k:k + LANES]
                 for i in range(0, both.shape[0], SUBLANES) for k in range(0, both.shape[1], LANES)]
        return functools.reduce(jnp.add, tiles)

    def matmuls(slot, norm_anchor=None):
        h = h_refs[slot][...]
        g = _dot(h, wg_ref[...])
        if norm_anchor is not None:
            filler = jnp.tile(norm_anchor[0:1, :], (1, g.shape[1] // LANES))
            g = jnp.where(r < 0, filler, g)
        u = _dot(h, wu_ref[...])
        act = (g * _sigmoid(g) * u).astype(_BF16)
        acc_refs[slot][...] += _dot(act, wd_ref[...])
        _cast_blocks(cast_in, cast_out)

    interior = jnp.logical_and(r >= 1, r <= n_tiles)
    has_norms = j < NORM_CHUNKS
    for slot in range(2):
        parity = jnp.logical_and(interior, r % 2 == slot)

        @pl.when(jnp.logical_and(parity, has_norms))
        def _(slot=slot):
            matmuls(1 - slot, norms(slot))

        @pl.when(jnp.logical_and(parity, jnp.logical_not(has_norms)))
        def _(slot=slot):
            matmuls(1 - slot)

    @pl.when(jnp.logical_and(r == 0, has_norms))
    def _():
        norms(0)

    @pl.when(jnp.logical_and(r > n_tiles, has_norms))
    def _():
        norms((n_tiles + 1) % 2)


def _ffn_call(x, gain, wg, wu, wd, post_gain, *, emit_residual, casts=(), tm=512, tf=512):
    t, d = x.shape
    f = wg.shape[1]
    n_tiles, n_f = t // tm, f // tf
    grid = (n_tiles + 2, n_f)
    assert n_f >= NORM_CHUNKS and tm % NORM_CHUNKS == 0
    vec_spec = pl.BlockSpec((1, d), lambda r, j: (0, 0))
    f_tile = lambda r, j: jnp.where(r == 0, 0, jnp.where(r > n_tiles, n_f - 1, j))
    in_specs = [
        pl.BlockSpec((tm, d), lambda r, j: (jnp.minimum(r, n_tiles - 1), 0)),
        vec_spec,
        pl.BlockSpec((d, tf), lambda r, j: (0, f_tile(r, j))),
        pl.BlockSpec((d, tf), lambda r, j: (0, f_tile(r, j))),
        pl.BlockSpec((tf, d), lambda r, j: (f_tile(r, j), 0)),
        vec_spec,
    ]
    out_spec = pl.BlockSpec((tm, d), lambda r, j: (jnp.maximum(r - 2, 0), 0))
    out_shape = [jax.ShapeDtypeStruct((t, d), _BF16 if emit_residual else _F32)]
    out_specs = [out_spec]
    if emit_residual:
        out_shape.insert(0, jax.ShapeDtypeStruct((t, d), _F32))
        out_specs.insert(0, out_spec)
    cast_specs, cast_shapes = _ride_along_casts(casts, grid, first_row=1, n_rows=n_tiles)
    in_specs += cast_specs
    out_specs += cast_specs
    out_shape += cast_shapes
    return pl.pallas_call(
        functools.partial(_ffn_kernel, emit_residual=emit_residual, n_casts=len(casts), n_tiles=n_tiles),
        grid=grid,
        in_specs=in_specs,
        out_specs=tuple(out_specs),
        out_shape=tuple(out_shape),
        scratch_shapes=[pltpu.VMEM((tm, d), _BF16)] * 2 + [pltpu.VMEM((tm, d), _F32)] * 2,
        compiler_params=pltpu.CompilerParams(
            dimension_semantics=("arbitrary", "arbitrary"), vmem_limit_bytes=VMEM_LIMIT_BYTES),
        name="ffn_residual" if emit_residual else "ffn_final",
    )(x, gain, wg, wu, wd, post_gain, *casts)


def _inproj_kernel(*refs, n_casts):
    u_ref, w_ref = refs[:2]
    cast_in = refs[2:2 + n_casts]
    mixer_refs = refs[2 + n_casts:5 + n_casts]
    cast_out = refs[5 + n_casts:]
    j = pl.program_id(1)
    for col, ref in enumerate(mixer_refs):
        @pl.when(j == col)
        def _(ref=ref):
            ref[...] = _dot(u_ref[...], w_ref[...]).astype(_BF16)
            _cast_blocks(cast_in, cast_out)


def _inproj_call(u, w_in, *, casts=(), tm=1024, tn=1024):
    t, d = u.shape
    assert FOURIER_WIDTH == LRU_WIDTH == tn
    grid = (t // tm, 3)
    mixer_spec = pl.BlockSpec((tm, tn), lambda i, j: (i, 0))
    cast_specs, cast_shapes = _ride_along_casts(casts, grid)
    return pl.pallas_call(
        functools.partial(_inproj_kernel, n_casts=len(casts)),
        grid=grid,
        in_specs=[
            pl.BlockSpec((tm, d), lambda i, j: (i, 0)),
            pl.BlockSpec((d, tn), lambda i, j: (0, j)),
        ] + cast_specs,
        out_specs=(mixer_spec, mixer_spec, mixer_spec, *cast_specs),
        out_shape=(
            jax.ShapeDtypeStruct((t, FOURIER_WIDTH), _BF16),
            jax.ShapeDtypeStruct((t, LRU_WIDTH), _BF16),
            jax.ShapeDtypeStruct((t, LRU_WIDTH), _BF16),
            *cast_shapes,
        ),
        compiler_params=pltpu.CompilerParams(
            dimension_semantics=("arbitrary", "arbitrary"), vmem_limit_bytes=VMEM_LIMIT_BYTES),
        name="inproj",
    )(u, w_in, *casts)


def _fourier_constants(seq):
    rows = seq // FFT_BLOCKS
    c = np.arange(GROUP_DIM)
    ang_c = 2.0 * np.pi * np.outer(c, c) / GROUP_DIM
    chan = np.concatenate([np.cos(ang_c), -np.sin(ang_c)], axis=1) / math.sqrt(GROUP_DIM)
    n1 = np.arange(rows)
    k1 = np.arange(rows)
    mats = []
    for k2 in range(FFT_BLOCKS):
        ang = 2.0 * np.pi * np.outer(FFT_BLOCKS * k1 + k2, n1) / seq
        mats.append(np.concatenate([np.cos(ang), np.sin(ang)], axis=1) / math.sqrt(seq))
    seq_mats = np.stack(mats)
    tile = FFT_BLOCKS * FFT_BLOCKS
    rho = np.arange(tile)
    perm = np.zeros((tile, tile), np.float32)
    perm[rho, FFT_BLOCKS * (rho % FFT_BLOCKS) + rho // FFT_BLOCKS] = 1.0
    return (jnp.asarray(chan, _F32), jnp.asarray(seq_mats, _F32), jnp.asarray(perm, _F32))


def _fft_across_blocks(xs):
    n = len(xs)
    if n == 1:
        return xs
    ev = _fft_across_blocks(xs[0::2])
    od = _fft_across_blocks(xs[1::2])
    out = [None] * n
    for k in range(n // 2):
        o_r, o_i = od[k]
        e_r, e_i = ev[k]
        if k == 0:
            t_r, t_i = o_r, o_i
        elif 4 * k == n:
            out[k] = (e_r + o_i, e_i - o_r)
            out[k + n // 2] = (e_r - o_i, e_i + o_r)
            continue
        else:
            c = math.cos(2.0 * math.pi * k / n)
            s = -math.sin(2.0 * math.pi * k / n)
            t_r = c * o_r - s * o_i
            t_i = c * o_i + s * o_r
        out[k] = (e_r + t_r, e_i + t_i)
        out[k + n // 2] = (e_r - t_r, e_i - t_i)
    return out


def _fourier_kernel(z_ref, chan_ref, seq_ref, perm_ref, o_ref, w_ref, b_ref, g_ref):
    seq = z_ref.shape[1]
    rows = seq // FFT_BLOCKS
    gd = GROUP_DIM

    for blk in range(FFT_BLOCKS):
        sl = pl.ds(blk * rows, rows)
        w_ref[sl, :] = _dot(z_ref[0, sl, :], chan_ref[...])

    def butterfly(r, carry):
        r0 = pl.multiple_of(r * SUBLANES, SUBLANES)
        for lc in range(gd // LANES):
            re_l = pl.ds(lc * LANES, LANES)
            im_l = pl.ds(gd + lc * LANES, LANES)
            xs = [(w_ref[pl.ds(blk * rows + r0, SUBLANES), re_l],
                   w_ref[pl.ds(blk * rows + r0, SUBLANES), im_l]) for blk in range(FFT_BLOCKS)]
            ys = _fft_across_blocks(xs)
            for k2 in range(FFT_BLOCKS):
                b_ref[k2, pl.ds(r0, SUBLANES), pl.ds(lc * LANES, LANES)] = ys[k2][0]
                b_ref[k2, pl.ds(rows + r0, SUBLANES), pl.ds(lc * LANES, LANES)] = ys[k2][1]
        return carry

    lax.fori_loop(0, rows // SUBLANES, butterfly, 0)

    for k2 in range(FFT_BLOCKS):
        yk = _dot(seq_ref[k2], b_ref[k2].astype(_BF16)).astype(_BF16)
        for a in range(rows // FFT_BLOCKS):
            g_ref[a, pl.ds(FFT_BLOCKS * k2, FFT_BLOCKS), :] = yk[FFT_BLOCKS * a:FFT_BLOCKS * (a + 1), :]

    tile = FFT_BLOCKS * FFT_BLOCKS
    for a in range(rows // FFT_BLOCKS):
        o_ref[0, pl.ds(a * tile, tile), :] = _dot(perm_ref[...], g_ref[a]).astype(o_ref.dtype)


def _fourier_call(z_four):
    b, s, w = z_four.shape
    rows = s // FFT_BLOCKS
    assert rows == FFT_BLOCKS * FFT_BLOCKS and w % GROUP_DIM == 0
    chan, seq_mats, perm = (c.astype(_BF16) for c in _fourier_constants(s))
    blk = pl.BlockSpec((1, s, GROUP_DIM), lambda i, g: (i, 0, g))
    return pl.pallas_call(
        _fourier_kernel,
        grid=(b, w // GROUP_DIM),
        in_specs=[
            blk,
            pl.BlockSpec(chan.shape, lambda i, g: (0, 0)),
            pl.BlockSpec(seq_mats.shape, lambda i, g: (0, 0, 0)),
            pl.BlockSpec(perm.shape, lambda i, g: (0, 0)),
        ],
        out_specs=blk,
        out_shape=jax.ShapeDtypeStruct((b, s, w), _BF16),
        scratch_shapes=[
            pltpu.VMEM((s, 2 * GROUP_DIM), _F32),
            pltpu.VMEM((FFT_BLOCKS, 2 * rows, GROUP_DIM), _F32),
            pltpu.VMEM((rows // FFT_BLOCKS, FFT_BLOCKS * FFT_BLOCKS, GROUP_DIM), _BF16),
        ],
        compiler_params=pltpu.CompilerParams(
            dimension_semantics=("parallel", "parallel"), vmem_limit_bytes=VMEM_LIMIT_BYTES),
        name="fourier_mix",
    )(z_four, chan, seq_mats, perm)


def _softplus(x):
    return jnp.maximum(x, 0.0) + jnp.log1p(jnp.exp(-jnp.abs(x)))


def _lru_kernel(zr_ref, zg_ref, cw_ref, cb_ref, wcat_ref, bcat_ref, lam_ref,
                tok_ref, gw0_ref, gw1_ref, gw2_ref, gw3_ref, gbias_ref, o_ref, gates_ref,
                zpad_ref, af_ref, uf_ref, ab_ref, ub_ref, hf_ref, hb_ref, tot_ref, cin_ref):
    seq = zr_ref.shape[1]
    hd = HEAD_DIM
    n_chunks = seq // SCAN_CHUNK
    n_groups = n_chunks // SUBLANES
    halo = SUBLANES

    def fill(c, carry):
        r0 = pl.multiple_of(c * SCAN_CHUNK, SCAN_CHUNK)
        p0 = pl.multiple_of(c * CONV_PITCH, SUBLANES)
        zpad_ref[pl.ds(p0 + halo, SCAN_CHUNK), :] = zr_ref[0, pl.ds(r0, SCAN_CHUNK), :].astype(_F32)
        return carry

    lax.fori_loop(0, n_chunks, fill, 0)
    for c in range(n_chunks):
        before = (zpad_ref[pl.ds((c - 1) * CONV_PITCH + SCAN_CHUNK, halo), :] if c > 0
                  else jnp.zeros((halo, hd), _F32))
        after = (zpad_ref[pl.ds((c + 1) * CONV_PITCH + halo, halo), :] if c < n_chunks - 1
                 else jnp.zeros((halo, hd), _F32))
        zpad_ref[pl.ds(c * CONV_PITCH, halo), :] = before
        zpad_ref[pl.ds(c * CONV_PITCH + halo + SCAN_CHUNK, halo), :] = after

    cw = 0.5 * cw_ref[...]
    cb = 0.5 * cb_ref[...]
    half_rate = (-0.5 * LRU_C * LOG2_E) * _softplus(-lam_ref[0])
    directions = ((af_ref, uf_ref, hf_ref), (ab_ref, ub_ref, hb_ref))

    def gates(tb):
        t0 = tb * GATE_STEPS
        taps = [zpad_ref[pl.ds(halo - 2 + t0 + m, n_chunks, stride=CONV_PITCH), :]
                for m in range(GATE_STEPS + 3)]
        hvs = [cb + sum(taps[j + k] * cw[k:k + 1, :] for k in range(4)) for j in range(GATE_STEPS)]
        g_half = _dot(jnp.concatenate(hvs, axis=0).astype(_BF16), wcat_ref[0])
        for j, hv in enumerate(hvs):
            rows = pl.ds((t0 + j) * n_chunks, n_chunks)
            th = jnp.tanh(g_half[j * n_chunks:(j + 1) * n_chunks, :] + bcat_ref[0])
            for d, (a_ref, u_ref, _) in enumerate(directions):
                th_r = th[:, (2 * d) * hd:(2 * d + 1) * hd]
                th_i = th[:, (2 * d + 1) * hd:(2 * d + 2) * hd]
                hr = half_rate[d:d + 1, :]
                log2_a = hr + hr * th_r
                a = jnp.exp2(log2_a)
                x = jnp.tanh(log2_a * (-LN_2)) * (1.0 + a * a)
                scale = jnp.where(x > 0.0, x * lax.rsqrt(x), 0.0)
                a_ref[rows, :] = a
                u_ref[rows, :] = scale * (1.0 + th_i) * hv

    def gate_piece(n):
        w_ref = (gw0_ref, gw1_ref, gw2_ref, gw3_ref)[n // pieces_per_ref]
        c0 = (n % pieces_per_ref) * GATE_PIECE
        cols = pl.ds(n * GATE_PIECE, GATE_PIECE)
        z = _dot(tok_ref[...], w_ref[:, c0:c0 + GATE_PIECE])
        for r in range(0, z.shape[0], PIECE_ROWS):
            gates_ref[pl.ds(r, PIECE_ROWS), cols] = _sigmoid(
                z[r:r + PIECE_ROWS, :] + gbias_ref[:, cols]).astype(gates_ref.dtype)

    n_pieces = gates_ref.shape[1] // GATE_PIECE
    pieces_per_ref = gw0_ref.shape[1] // GATE_PIECE
    n_blocks = SCAN_CHUNK // GATE_STEPS
    assert n_blocks % n_pieces == 0
    for tb in range(n_blocks):
        gates(tb)
        if (tb + 1) % (n_blocks // n_pieces) == 0:
            gate_piece(tb // (n_blocks // n_pieces))

    def step_rows(d, g, i):
        t = i if d == 0 else SCAN_CHUNK - 1 - i
        return pl.ds(pl.multiple_of(t * n_chunks, n_chunks) + g * SUBLANES, SUBLANES)

    def chunk_rows(d, g, i):
        t = i if d == 0 else SCAN_CHUNK - 1 - i
        return pl.ds(g * SUBLANES * SCAN_PITCH + t, SUBLANES, stride=SCAN_PITCH)

    lanes = [(d, g) for d in range(2) for g in range(n_groups)]

    def totals(i, carry):
        out = []
        for (d, g), (prod, resp) in zip(lanes, carry):
            a = directions[d][0][step_rows(d, g, i), :]
            u = directions[d][1][step_rows(d, g, i), :]
            out.append((prod * a, a * resp + u))
        return tuple(out)

    init = tuple((jnp.ones((SUBLANES, hd), _F32), jnp.zeros((SUBLANES, hd), _F32)) for _ in lanes)
    tot = lax.fori_loop(0, SCAN_CHUNK, totals, init, unroll=SCAN_UNROLL)
    for (d, g), (prod, resp) in zip(lanes, tot):
        tot_ref[2 * d, pl.ds(g * SUBLANES, SUBLANES), :] = prod
        tot_ref[2 * d + 1, pl.ds(g * SUBLANES, SUBLANES), :] = resp

    for d in range(2):
        state = jnp.zeros((1, hd), _F32)
        for c in (range(n_chunks) if d == 0 else range(n_chunks - 1, -1, -1)):
            cin_ref[d, pl.ds(c, 1), :] = state
            state = tot_ref[2 * d, pl.ds(c, 1), :] * state + tot_ref[2 * d + 1, pl.ds(c, 1), :]

    def replay(i, carry):
        out = []
        for (d, g), h_in in zip(lanes, carry):
            a = directions[d][0][step_rows(d, g, i), :]
            u = directions[d][1][step_rows(d, g, i), :]
            h = a * h_in + u
            directions[d][2][chunk_rows(d, g, i), :] = h
            out.append(h)
        return tuple(out)

    lax.fori_loop(0, SCAN_CHUNK, replay,
                  tuple(cin_ref[d, pl.ds(g * SUBLANES, SUBLANES), :] for d, g in lanes), unroll=SCAN_UNROLL)

    def finish(c, carry):
        r0 = pl.multiple_of(c * SCAN_CHUNK, SCAN_CHUNK)
        p0 = pl.multiple_of(c * SCAN_PITCH, SUBLANES)
        h = hf_ref[pl.ds(p0, SCAN_CHUNK), :] + hb_ref[pl.ds(p0, SCAN_CHUNK), :]
        zg = zg_ref[0, pl.ds(r0, SCAN_CHUNK), :].astype(_F32)
        o_ref[0, pl.ds(r0, SCAN_CHUNK), :] = (h * _gelu_tanh(zg)).astype(o_ref.dtype)
        return carry

    lax.fori_loop(0, n_chunks, finish, 0)


def _lru_call(z_rec, z_gelu, conv_w, conv_b, wcat, bcat, lam, tokens, w_in, b_gates):
    b, s, w = z_rec.shape
    heads = w // HEAD_DIM
    n_chunks = s // SCAN_CHUNK
    assert s % (SCAN_CHUNK * SUBLANES) == 0
    t, d = tokens.shape
    tile = t // (b * heads)
    n_wrefs = 4
    wcols = GATE_WIDTH // n_wrefs
    first = (IN_WIDTH - GATE_WIDTH) // wcols
    assert (IN_WIDTH - GATE_WIDTH) % wcols == 0 and t % (b * heads) == 0
    blk = pl.BlockSpec((1, s, HEAD_DIM), lambda i, h: (i, 0, h))
    pitched = pltpu.VMEM((n_chunks * SCAN_PITCH, HEAD_DIM), _F32)
    time_major = pltpu.VMEM((s, HEAD_DIM), _F32)
    gate_w_specs = [
        pl.BlockSpec((d, wcols), lambda i, h, c=first + k: (0, c), pipeline_mode=pl.Buffered(1))
        for k in range(n_wrefs)]
    return pl.pallas_call(
        _lru_kernel,
        grid=(b, heads),
        in_specs=[
            blk,
            blk,
            pl.BlockSpec((conv_w.shape[0], HEAD_DIM), lambda i, h: (0, h)),
            pl.BlockSpec((1, HEAD_DIM), lambda i, h: (0, h)),
            pl.BlockSpec((1, HEAD_DIM, 4 * HEAD_DIM), lambda i, h: (h, 0, 0)),
            pl.BlockSpec((1, 1, 4 * HEAD_DIM), lambda i, h: (h, 0, 0)),
            pl.BlockSpec((1, 2, HEAD_DIM), lambda i, h: (h, 0, 0)),
            pl.BlockSpec((tile, d), lambda i, h: (i * heads + h, 0)),
            *gate_w_specs,
            pl.BlockSpec((1, GATE_WIDTH), lambda i, h: (0, 0)),
        ],
        out_specs=(blk, pl.BlockSpec((tile, GATE_WIDTH), lambda i, h: (i * heads + h, 0))),
        out_shape=(jax.ShapeDtypeStruct((b, s, w), _BF16), jax.ShapeDtypeStruct((t, GATE_WIDTH), _BF16)),
        scratch_shapes=[
            pltpu.VMEM((n_chunks * CONV_PITCH, HEAD_DIM), _F32),
            time_major, time_major, time_major, time_major, pitched, pitched,
            pltpu.VMEM((4, n_chunks, HEAD_DIM), _F32),
            pltpu.VMEM((2, n_chunks, HEAD_DIM), _F32),
        ],
        compiler_params=pltpu.CompilerParams(
            dimension_semantics=("parallel", "parallel"), vmem_limit_bytes=VMEM_LIMIT_BYTES),
        name="rglru_gates",
    )(z_rec, z_gelu, conv_w, conv_b, wcat, bcat, lam, tokens, w_in, w_in, w_in, w_in, b_gates)


def _merge_kernel(x_ref, yf_ref, yg_ref, gates_ref, pa_ref, pb_ref, wo_ref, o_ref):
    d = x_ref.shape[1]
    y_a = _dot(yf_ref[...], pa_ref[...])
    y_b = _dot(yg_ref[...], pb_ref[...])
    merged = gates_ref[:, :d].astype(_F32) * y_a + gates_ref[:, d:].astype(_F32) * y_b
    o_ref[...] = x_ref[...] + _dot(merged.astype(_BF16), wo_ref[...])


def _merge_call(x, y_four, y_rec, gates, proj_a, proj_b, w_out, *, tm=256):
    t, d = x.shape
    resident = functools.partial(pl.BlockSpec, index_map=lambda i: (0, 0), pipeline_mode=pl.Buffered(1))
    return pl.pallas_call(
        _merge_kernel,
        grid=(t // tm,),
        in_specs=[
            pl.BlockSpec((tm, d), lambda i: (i, 0)),
            pl.BlockSpec((tm, y_four.shape[1]), lambda i: (i, 0)),
            pl.BlockSpec((tm, y_rec.shape[1]), lambda i: (i, 0)),
            pl.BlockSpec((tm, gates.shape[1]), lambda i: (i, 0)),
            resident(proj_a.shape),
            resident(proj_b.shape),
            resident(w_out.shape),
        ],
        out_specs=pl.BlockSpec((tm, d), lambda i: (i, 0)),
        out_shape=jax.ShapeDtypeStruct((t, d), _F32),
        compiler_params=pltpu.CompilerParams(
            dimension_semantics=("parallel",), vmem_limit_bytes=VMEM_LIMIT_BYTES),
        name="merge",
    )(x, y_four, y_rec, gates, proj_a, proj_b, w_out)


def kernel(x, ffn1_norm, ffn1_w_gate, ffn1_w_up, ffn1_w_down, mix_norm, w_in, b_gates, conv_w, conv_b, lru_wa, lru_ba, lru_wx, lru_bx, lru_lambda, proj_a, proj_b, w_out, ffn2_norm, ffn2_w_gate, ffn2_w_up, ffn2_w_down, final_norm):
    b, s, d = x.shape
    assert ffn1_norm.shape[0] == 1, "single-layer problem"
    bf = lambda w: w.astype(_BF16)
    x1, u, w_in_b = _ffn_call(
        x.reshape(b * s, d), ffn1_norm, bf(ffn1_w_gate[0]), bf(ffn1_w_up[0]), bf(ffn1_w_down[0]),
        mix_norm, emit_residual=True, casts=(w_in[0],))
    z_four, z_rec, z_gelu, wg2, wu2, wd2, proj_a_b, proj_b_b, w_out_b = _inproj_call(
        u, w_in_b, casts=(ffn2_w_gate[0], ffn2_w_up[0], ffn2_w_down[0], proj_a[0], proj_b[0], w_out[0]))

    y_four = _fourier_call(z_four.reshape(b, s, FOURIER_WIDTH))

    wcat = jnp.concatenate([lru_wa[0, 0], lru_wx[0, 0], lru_wa[0, 1], lru_wx[0, 1]], axis=-1)
    bcat = 0.5 * jnp.concatenate([lru_ba[0, 0], lru_bx[0, 0], lru_ba[0, 1], lru_bx[0, 1]], axis=-1)
    lam = jnp.transpose(lru_lambda[0], (1, 0, 2))
    y_rec, gates = _lru_call(z_rec.reshape(b, s, LRU_WIDTH), z_gelu.reshape(b, s, LRU_WIDTH),
                             conv_w[0], conv_b, bf(wcat), bcat[:, None, :], lam,
                             u, w_in_b, b_gates.reshape(1, GATE_WIDTH))

    x2 = _merge_call(x1, y_four.reshape(b * s, FOURIER_WIDTH), y_rec.reshape(b * s, LRU_WIDTH), gates,
                     proj_a_b, proj_b_b, w_out_b)
    (out,) = _ffn_call(x2, ffn2_norm, wg2, wu2, wd2, final_norm[None], emit_residual=False)
    return out.reshape(b, s, d)
```

```python
import functools
import math

import numpy as np
import jax
import jax.numpy as jnp
from jax import lax
from jax.experimental import pallas as pl
from jax.experimental.pallas import tpu as pltpu

D_MODEL = 2048
D_FF = 5632
FOURIER_WIDTH = 1024
FOURIER_GROUPS = 4
GROUP_DIM = FOURIER_WIDTH // FOURIER_GROUPS
LRU_WIDTH = 1024
LRU_HEADS = 8
HEAD_DIM = LRU_WIDTH // LRU_HEADS
LRU_C = 8.0
RMS_EPS = 1e-6
FFN_RES_SCALE = 0.5
GATE_WIDTH = 2 * D_MODEL
IN_WIDTH = FOURIER_WIDTH + 2 * LRU_WIDTH + GATE_WIDTH

VMEM_LIMIT_BYTES = 56 * 1024 * 1024
SUBLANES = 8
LANES = 128
CAST_ROWS = 2 * SUBLANES

FFT_BLOCKS = 16
SCAN_CHUNK = 128
SCAN_PITCH = SCAN_CHUNK + SUBLANES
CONV_PITCH = SCAN_CHUNK + 3 * SUBLANES
PIECE_ROWS = 64
GATE_STEPS = 8
GATE_PIECE = 256
SCAN_UNROLL = 4
GATE_LOOKAHEAD = 1
LOG2_E = 1.0 / math.log(2.0)
LN_2 = math.log(2.0)

_BF16 = jnp.bfloat16
_F32 = jnp.float32


def _dot(a, b):
    return jnp.dot(a, b, preferred_element_type=_F32)


def _sigmoid(x):
    return 0.5 * (1.0 + jnp.tanh(0.5 * x))


def _rms_norm(x, gain):
    ms = jnp.mean(x * x, axis=-1, keepdims=True)
    return x * lax.rsqrt(ms + RMS_EPS) * gain


def _gelu_tanh(x):
    return 0.5 * x * (1.0 + jnp.tanh(math.sqrt(2.0 / math.pi) * (x + 0.044715 * (x * x * x))))


def _ride_along_casts(weights, grid, first_row=0, n_rows=None):
    n_rows = grid[0] if n_rows is None else n_rows
    n_steps = n_rows * grid[1]
    specs, shapes = [], []
    for w in weights:
        rows, cols = w.shape
        block_rows = next(r for r in range(CAST_ROWS, rows + 1, CAST_ROWS)
                          if rows % r == 0 and rows // r <= n_steps)
        n_blocks = rows // block_rows

        def index_map(i, j, n_blocks=n_blocks):
            step = jnp.clip((i - first_row) * grid[1] + j, 0, n_steps - 1)
            return (step * n_blocks // n_steps, 0)

        specs.append(pl.BlockSpec((block_rows, cols), index_map))
        shapes.append(jax.ShapeDtypeStruct((rows, cols), _BF16))
    return specs, shapes


def _cast_blocks(srcs, dsts):
    for src, dst in zip(srcs, dsts):
        dst[...] = src[...].astype(_BF16)


def _ffn_kernel(*refs, emit_residual, n_casts):
    x_ref, gain_ref, wg_ref, wu_ref, wd_ref, post_ref = refs[:6]
    cast_in = refs[6:6 + n_casts]
    outs = refs[6 + n_casts:]
    if emit_residual:
        res_ref, normed_ref = outs[:2]
        outs = outs[2:]
    else:
        normed_ref = outs[0]
        outs = outs[1:]
    cast_out = outs[:n_casts]
    (h_ref,) = outs[n_casts:]
    acc_ref = res_ref if emit_residual else normed_ref
    j = pl.program_id(1)

    @pl.when(j == 0)
    def _():
        h_ref[...] = _rms_norm(x_ref[...], gain_ref[...]).astype(_BF16)
        acc_ref[...] = jnp.zeros_like(acc_ref)

    h = h_ref[...]
    g = _dot(h, wg_ref[...])
    u = _dot(h, wu_ref[...])
    act = (g * _sigmoid(g) * u).astype(_BF16)
    acc_ref[...] += _dot(act, wd_ref[...])

    _cast_blocks(cast_in, cast_out)

    @pl.when(j == pl.num_programs(1) - 1)
    def _():
        y = x_ref[...] + FFN_RES_SCALE * acc_ref[...]
        if emit_residual:
            res_ref[...] = y
        normed_ref[...] = _rms_norm(y, post_ref[...]).astype(normed_ref.dtype)


def _ffn_call(x, gain, wg, wu, wd, post_gain, *, emit_residual, casts=(), tm, tf=512):
    t, d = x.shape
    f = wg.shape[1]
    grid = (t // tm, f // tf)
    row_spec = pl.BlockSpec((tm, d), lambda i, j: (i, 0))
    vec_spec = pl.BlockSpec((1, d), lambda i, j: (0, 0))
    in_specs = [
        row_spec,
        vec_spec,
        pl.BlockSpec((d, tf), lambda i, j: (0, j)),
        pl.BlockSpec((d, tf), lambda i, j: (0, j)),
        pl.BlockSpec((tf, d), lambda i, j: (j, 0)),
        vec_spec,
    ]
    out_shape = [jax.ShapeDtypeStruct((t, d), _BF16 if emit_residual else _F32)]
    out_specs = [row_spec]
    if emit_residual:
        out_shape.insert(0, jax.ShapeDtypeStruct((t, d), _F32))
        out_specs.insert(0, row_spec)
    cast_specs, cast_shapes = _ride_along_casts(casts, grid)
    in_specs += cast_specs
    out_specs += cast_specs
    out_shape += cast_shapes
    return pl.pallas_call(
        functools.partial(_ffn_kernel, emit_residual=emit_residual, n_casts=len(casts)),
        grid=grid,
        in_specs=in_specs,
        out_specs=tuple(out_specs),
        out_shape=tuple(out_shape),
        scratch_shapes=[pltpu.VMEM((tm, d), _BF16)],
        compiler_params=pltpu.CompilerParams(
            dimension_semantics=("arbitrary", "arbitrary"), vmem_limit_bytes=VMEM_LIMIT_BYTES),
        name="ffn_residual" if emit_residual else "ffn_final",
    )(x, gain, wg, wu, wd, post_gain, *casts)


def _inproj_kernel(*refs, n_casts):
    u_ref, w_ref = refs[:2]
    cast_in = refs[2:2 + n_casts]
    mixer_refs = refs[2 + n_casts:5 + n_casts]
    cast_out = refs[5 + n_casts:]
    j = pl.program_id(1)
    for col, ref in enumerate(mixer_refs):
        @pl.when(j == col)
        def _(ref=ref):
            ref[...] = _dot(u_ref[...], w_ref[...]).astype(_BF16)
            _cast_blocks(cast_in, cast_out)


def _inproj_call(u, w_in, *, casts=(), tm=1024, tn=1024):
    t, d = u.shape
    assert FOURIER_WIDTH == LRU_WIDTH == tn
    grid = (t // tm, 3)
    mixer_spec = pl.BlockSpec((tm, tn), lambda i, j: (i, 0))
    cast_specs, cast_shapes = _ride_along_casts(casts, grid)
    return pl.pallas_call(
        functools.partial(_inproj_kernel, n_casts=len(casts)),
        grid=grid,
        in_specs=[
            pl.BlockSpec((tm, d), lambda i, j: (i, 0)),
            pl.BlockSpec((d, tn), lambda i, j: (0, j)),
        ] + cast_specs,
        out_specs=(mixer_spec, mixer_spec, mixer_spec, *cast_specs),
        out_shape=(
            jax.ShapeDtypeStruct((t, FOURIER_WIDTH), _BF16),
            jax.ShapeDtypeStruct((t, LRU_WIDTH), _BF16),
            jax.ShapeDtypeStruct((t, LRU_WIDTH), _BF16),
            *cast_shapes,
        ),
        compiler_params=pltpu.CompilerParams(
            dimension_semantics=("arbitrary", "arbitrary"), vmem_limit_bytes=VMEM_LIMIT_BYTES),
        name="inproj",
    )(u, w_in, *casts)


def _fourier_constants(seq):
    rows = seq // FFT_BLOCKS
    c = np.arange(GROUP_DIM)
    ang_c = 2.0 * np.pi * np.outer(c, c) / GROUP_DIM
    chan = np.concatenate([np.cos(ang_c), -np.sin(ang_c)], axis=1) / math.sqrt(GROUP_DIM)
    n1 = np.arange(rows)
    k1 = np.arange(rows)
    mats = []
    for k2 in range(FFT_BLOCKS):
        ang = 2.0 * np.pi * np.outer(FFT_BLOCKS * k1 + k2, n1) / seq
        mats.append(np.concatenate([np.cos(ang), np.sin(ang)], axis=1) / math.sqrt(seq))
    seq_mats = np.stack(mats)
    tile = FFT_BLOCKS * FFT_BLOCKS
    rho = np.arange(tile)
    perm = np.zeros((tile, tile), np.float32)
    perm[rho, FFT_BLOCKS * (rho % FFT_BLOCKS) + rho // FFT_BLOCKS] = 1.0
    return (jnp.asarray(chan, _F32), jnp.asarray(seq_mats, _F32), jnp.asarray(perm, _F32))


def _fft_across_blocks(xs):
    n = len(xs)
    if n == 1:
        return xs
    ev = _fft_across_blocks(xs[0::2])
    od = _fft_across_blocks(xs[1::2])
    out = [None] * n
    for k in range(n // 2):
        o_r, o_i = od[k]
        e_r, e_i = ev[k]
        if k == 0:
            t_r, t_i = o_r, o_i
        elif 4 * k == n:
            out[k] = (e_r + o_i, e_i - o_r)
            out[k + n // 2] = (e_r - o_i, e_i + o_r)
            continue
        else:
            c = math.cos(2.0 * math.pi * k / n)
            s = -math.sin(2.0 * math.pi * k / n)
            t_r = c * o_r - s * o_i
            t_i = c * o_i + s * o_r
        out[k] = (e_r + t_r, e_i + t_i)
        out[k + n // 2] = (e_r - t_r, e_i - t_i)
    return out


def _fourier_kernel(z_ref, chan_ref, seq_ref, perm_ref, o_ref, w_ref, b_ref, g_ref):
    seq = z_ref.shape[1]
    rows = seq // FFT_BLOCKS
    gd = GROUP_DIM

    for blk in range(FFT_BLOCKS):
        sl = pl.ds(blk * rows, rows)
        w_ref[sl, :] = _dot(z_ref[0, sl, :], chan_ref[...])

    def butterfly(r, carry):
        r0 = pl.multiple_of(r * SUBLANES, SUBLANES)
        for lc in range(gd // LANES):
            re_l = pl.ds(lc * LANES, LANES)
            im_l = pl.ds(gd + lc * LANES, LANES)
            xs = [(w_ref[pl.ds(blk * rows + r0, SUBLANES), re_l],
                   w_ref[pl.ds(blk * rows + r0, SUBLANES), im_l]) for blk in range(FFT_BLOCKS)]
            ys = _fft_across_blocks(xs)
            for k2 in range(FFT_BLOCKS):
                b_ref[k2, pl.ds(r0, SUBLANES), pl.ds(lc * LANES, LANES)] = ys[k2][0]
                b_ref[k2, pl.ds(rows + r0, SUBLANES), pl.ds(lc * LANES, LANES)] = ys[k2][1]
        return carry

    lax.fori_loop(0, rows // SUBLANES, butterfly, 0, unroll=2)

    for k2 in range(FFT_BLOCKS):
        yk = _dot(seq_ref[k2], b_ref[k2].astype(_BF16)).astype(_BF16)
        for a in range(rows // FFT_BLOCKS):
            g_ref[a, pl.ds(FFT_BLOCKS * k2, FFT_BLOCKS), :] = yk[FFT_BLOCKS * a:FFT_BLOCKS * (a + 1), :]

    tile = FFT_BLOCKS * FFT_BLOCKS
    for a in range(rows // FFT_BLOCKS):
        o_ref[0, pl.ds(a * tile, tile), :] = _dot(perm_ref[...], g_ref[a]).astype(o_ref.dtype)


def _fourier_call(z_four):
    b, s, w = z_four.shape
    rows = s // FFT_BLOCKS
    assert rows == FFT_BLOCKS * FFT_BLOCKS and w % GROUP_DIM == 0
    chan, seq_mats, perm = (c.astype(_BF16) for c in _fourier_constants(s))
    blk = pl.BlockSpec((1, s, GROUP_DIM), lambda i, g: (i, 0, g))
    return pl.pallas_call(
        _fourier_kernel,
        grid=(b, w // GROUP_DIM),
        in_specs=[
            blk,
            pl.BlockSpec(chan.shape, lambda i, g: (0, 0)),
            pl.BlockSpec(seq_mats.shape, lambda i, g: (0, 0, 0)),
            pl.BlockSpec(perm.shape, lambda i, g: (0, 0)),
        ],
        out_specs=blk,
        out_shape=jax.ShapeDtypeStruct((b, s, w), _BF16),
        scratch_shapes=[
            pltpu.VMEM((s, 2 * GROUP_DIM), _F32),
            pltpu.VMEM((FFT_BLOCKS, 2 * rows, GROUP_DIM), _F32),
            pltpu.VMEM((rows // FFT_BLOCKS, FFT_BLOCKS * FFT_BLOCKS, GROUP_DIM), _BF16),
        ],
        compiler_params=pltpu.CompilerParams(
            dimension_semantics=("parallel", "parallel"), vmem_limit_bytes=VMEM_LIMIT_BYTES),
        name="fourier_mix",
    )(z_four, chan, seq_mats, perm)


def _softplus(x):
    return jnp.maximum(x, 0.0) + jnp.log1p(jnp.exp(-jnp.abs(x)))


def _lru_kernel(zr_ref, zg_ref, cw_ref, cb_ref, wcat_ref, bcat_ref, lam_ref,
                tok_ref, gw0_ref, gw1_ref, gw2_ref, gw3_ref, gbias_ref, o_ref, gates_ref,
                zpad_ref, af_ref, uf_ref, ab_ref, ub_ref, hf_ref, hb_ref, tot_ref, cin_ref):
    seq = zr_ref.shape[1]
    hd = HEAD_DIM
    n_chunks = seq // SCAN_CHUNK
    n_groups = n_chunks // SUBLANES
    halo = SUBLANES

    def fill(c, carry):
        r0 = pl.multiple_of(c * SCAN_CHUNK, SCAN_CHUNK)
        p0 = pl.multiple_of(c * CONV_PITCH, SUBLANES)
        zpad_ref[pl.ds(p0 + halo, SCAN_CHUNK), :] = zr_ref[0, pl.ds(r0, SCAN_CHUNK), :].astype(_F32)
        return carry

    lax.fori_loop(0, n_chunks, fill, 0)
    for c in range(n_chunks):
        before = (zpad_ref[pl.ds((c - 1) * CONV_PITCH + SCAN_CHUNK, halo), :] if c > 0
                  else jnp.zeros((halo, hd), _F32))
        after = (zpad_ref[pl.ds((c + 1) * CONV_PITCH + halo, halo), :] if c < n_chunks - 1
                 else jnp.zeros((halo, hd), _F32))
        zpad_ref[pl.ds(c * CONV_PITCH, halo), :] = before
        zpad_ref[pl.ds(c * CONV_PITCH + halo + SCAN_CHUNK, halo), :] = after

    cw = 0.5 * cw_ref[...]
    cb = 0.5 * cb_ref[...]
    half_rate = (-0.5 * LRU_C * LOG2_E) * _softplus(-lam_ref[0])
    directions = ((af_ref, uf_ref, hf_ref), (ab_ref, ub_ref, hb_ref))

    def gate_logits(tb):
        t0 = tb * GATE_STEPS
        taps = [zpad_ref[pl.ds(halo - 2 + t0 + m, n_chunks, stride=CONV_PITCH), :]
                for m in range(GATE_STEPS + 3)]
        hvs = [cb + sum(taps[j + k] * cw[k:k + 1, :] for k in range(4)) for j in range(GATE_STEPS)]
        return hvs, _dot(jnp.concatenate(hvs, axis=0).astype(_BF16), wcat_ref[0])

    def gates(tb, hvs, g_half):
        t0 = tb * GATE_STEPS
        for j, hv in enumerate(hvs):
            rows = pl.ds((t0 + j) * n_chunks, n_chunks)
            th = jnp.tanh(g_half[j * n_chunks:(j + 1) * n_chunks, :] + bcat_ref[0])
            for d, (a_ref, u_ref, _) in enumerate(directions):
                th_r = th[:, (2 * d) * hd:(2 * d + 1) * hd]
                th_i = th[:, (2 * d + 1) * hd:(2 * d + 2) * hd]
                hr = half_rate[d:d + 1, :]
                log2_a = hr + hr * th_r
                a = jnp.exp2(log2_a)
                x = jnp.tanh(log2_a * (-LN_2)) * (1.0 + a * a)
                scale = jnp.where(x > 0.0, x * lax.rsqrt(x), 0.0)
                a_ref[rows, :] = a
                u_ref[rows, :] = scale * (1.0 + th_i) * hv

    def gate_piece(n):
        w_ref = (gw0_ref, gw1_ref, gw2_ref, gw3_ref)[n // pieces_per_ref]
        c0 = (n % pieces_per_ref) * GATE_PIECE
        cols = pl.ds(n * GATE_PIECE, GATE_PIECE)
        z = _dot(tok_ref[...], w_ref[:, c0:c0 + GATE_PIECE])
        for r in range(0, z.shape[0], PIECE_ROWS):
            gates_ref[pl.ds(r, PIECE_ROWS), cols] = _sigmoid(
                z[r:r + PIECE_ROWS, :] + gbias_ref[:, cols]).astype(gates_ref.dtype)

    n_pieces = gates_ref.shape[1] // GATE_PIECE
    pieces_per_ref = gw0_ref.shape[1] // GATE_PIECE
    n_blocks = SCAN_CHUNK // GATE_STEPS
    assert n_blocks % n_pieces == 0
    logits = [gate_logits(tb) for tb in range(GATE_LOOKAHEAD)]
    for tb in range(n_blocks):
        if tb + GATE_LOOKAHEAD < n_blocks:
            logits.append(gate_logits(tb + GATE_LOOKAHEAD))
        gates(tb, *logits.pop(0))
        if (tb + 1) % (n_blocks // n_pieces) == 0:
            gate_piece(tb // (n_blocks // n_pieces))

    def step_rows(d, g, i):
        t = i if d == 0 else SCAN_CHUNK - 1 - i
        return pl.ds(pl.multiple_of(t * n_chunks, n_chunks) + g * SUBLANES, SUBLANES)

    def chunk_rows(d, g, i):
        t = i if d == 0 else SCAN_CHUNK - 1 - i
        return pl.ds(g * SUBLANES * SCAN_PITCH + t, SUBLANES, stride=SCAN_PITCH)

    lanes = [(d, g) for d in range(2) for g in range(n_groups)]

    def totals(i, carry):
        out = []
        for (d, g), (prod, resp) in zip(lanes, carry):
            a = directions[d][0][step_rows(d, g, i), :]
            u = directions[d][1][step_rows(d, g, i), :]
            out.append((prod * a, a * resp + u))
        return tuple(out)

    init = tuple((jnp.ones((SUBLANES, hd), _F32), jnp.zeros((SUBLANES, hd), _F32)) for _ in lanes)
    tot = lax.fori_loop(0, SCAN_CHUNK, totals, init, unroll=SCAN_UNROLL)
    for (d, g), (prod, resp) in zip(lanes, tot):
        tot_ref[2 * d, pl.ds(g * SUBLANES, SUBLANES), :] = prod
        tot_ref[2 * d + 1, pl.ds(g * SUBLANES, SUBLANES), :] = resp

    for d in range(2):
        state = jnp.zeros((1, hd), _F32)
        for c in (range(n_chunks) if d == 0 else range(n_chunks - 1, -1, -1)):
            cin_ref[d, pl.ds(c, 1), :] = state
            state = tot_ref[2 * d, pl.ds(c, 1), :] * state + tot_ref[2 * d + 1, pl.ds(c, 1), :]

    def replay(i, carry):
        out = []
        for (d, g), h_in in zip(lanes, carry):
            a = directions[d][0][step_rows(d, g, i), :]
            u = directions[d][1][step_rows(d, g, i), :]
            h = a * h_in + u
            directions[d][2][chunk_rows(d, g, i), :] = h
            out.append(h)
        return tuple(out)

    lax.fori_loop(0, SCAN_CHUNK, replay,
                  tuple(cin_ref[d, pl.ds(g * SUBLANES, SUBLANES), :] for d, g in lanes), unroll=SCAN_UNROLL)

    def finish(c, carry):
        r0 = pl.multiple_of(c * SCAN_CHUNK, SCAN_CHUNK)
        p0 = pl.multiple_of(c * SCAN_PITCH, SUBLANES)
        h = hf_ref[pl.ds(p0, SCAN_CHUNK), :] + hb_ref[pl.ds(p0, SCAN_CHUNK), :]
        zg = zg_ref[0, pl.ds(r0, SCAN_CHUNK), :].astype(_F32)
        o_ref[0, pl.ds(r0, SCAN_CHUNK), :] = (h * _gelu_tanh(zg)).astype(o_ref.dtype)
        return carry

    lax.fori_loop(0, n_chunks, finish, 0)


def _lru_call(z_rec, z_gelu, conv_w, conv_b, wcat, bcat, lam, tokens, w_in, b_gates):
    b, s, w = z_rec.shape
    heads = w // HEAD_DIM
    n_chunks = s // SCAN_CHUNK
    assert s % (SCAN_CHUNK * SUBLANES) == 0
    t, d = tokens.shape
    tile = t // (b * heads)
    n_wrefs = 4
    wcols = GATE_WIDTH // n_wrefs
    first = (IN_WIDTH - GATE_WIDTH) // wcols
    assert (IN_WIDTH - GATE_WIDTH) % wcols == 0 and t % (b * heads) == 0
    blk = pl.BlockSpec((1, s, HEAD_DIM), lambda i, h: (i, 0, h))
    pitched = pltpu.VMEM((n_chunks * SCAN_PITCH, HEAD_DIM), _F32)
    time_major = pltpu.VMEM((s, HEAD_DIM), _F32)
    gate_w_specs = [
        pl.BlockSpec((d, wcols), lambda i, h, c=first + k: (0, c), pipeline_mode=pl.Buffered(1))
        for k in range(n_wrefs)]
    return pl.pallas_call(
        _lru_kernel,
        grid=(b, heads),
        in_specs=[
            blk,
            blk,
            pl.BlockSpec((conv_w.shape[0], HEAD_DIM), lambda i, h: (0, h)),
            pl.BlockSpec((1, HEAD_DIM), lambda i, h: (0, h)),
            pl.BlockSpec((1, HEAD_DIM, 4 * HEAD_DIM), lambda i, h: (h, 0, 0)),
            pl.BlockSpec((1, 1, 4 * HEAD_DIM), lambda i, h: (h, 0, 0)),
            pl.BlockSpec((1, 2, HEAD_DIM), lambda i, h: (h, 0, 0)),
            pl.BlockSpec((tile, d), lambda i, h: (i * heads + h, 0)),
            *gate_w_specs,
            pl.BlockSpec((1, GATE_WIDTH), lambda i, h: (0, 0)),
        ],
        out_specs=(blk, pl.BlockSpec((tile, GATE_WIDTH), lambda i, h: (i * heads + h, 0))),
        out_shape=(jax.ShapeDtypeStruct((b, s, w), _BF16), jax.ShapeDtypeStruct((t, GATE_WIDTH), _BF16)),
        scratch_shapes=[
            pltpu.VMEM((n_chunks * CONV_PITCH, HEAD_DIM), _F32),
            time_major, time_major, time_major, time_major, pitched, pitched,
            pltpu.VMEM((4, n_chunks, HEAD_DIM), _F32),
            pltpu.VMEM((2, n_chunks, HEAD_DIM), _F32),
        ],
        compiler_params=pltpu.CompilerParams(
            dimension_semantics=("parallel", "parallel"), vmem_limit_bytes=VMEM_LIMIT_BYTES),
        name="rglru_gates",
    )(z_rec, z_gelu, conv_w, conv_b, wcat, bcat, lam, tokens, w_in, w_in, w_in, w_in, b_gates)


def _merge_kernel(x_ref, yf_ref, yg_ref, gates_ref, pa_ref, pb_ref, wo_ref, o_ref):
    d = x_ref.shape[1]
    y_a = _dot(yf_ref[...], pa_ref[...])
    y_b = _dot(yg_ref[...], pb_ref[...])
    merged = gates_ref[:, :d].astype(_F32) * y_a + gates_ref[:, d:].astype(_F32) * y_b
    o_ref[...] = x_ref[...] + _dot(merged.astype(_BF16), wo_ref[...])


def _merge_call(x, y_four, y_rec, gates, proj_a, proj_b, w_out, *, tm=256):
    t, d = x.shape
    resident = functools.partial(pl.BlockSpec, index_map=lambda i: (0, 0), pipeline_mode=pl.Buffered(1))
    return pl.pallas_call(
        _merge_kernel,
        grid=(t // tm,),
        in_specs=[
            pl.BlockSpec((tm, d), lambda i: (i, 0)),
            pl.BlockSpec((tm, y_four.shape[1]), lambda i: (i, 0)),
            pl.BlockSpec((tm, y_rec.shape[1]), lambda i: (i, 0)),
            pl.BlockSpec((tm, gates.shape[1]), lambda i: (i, 0)),
            resident(proj_a.shape),
            resident(proj_b.shape),
            resident(w_out.shape),
        ],
        out_specs=pl.BlockSpec((tm, d), lambda i: (i, 0)),
        out_shape=jax.ShapeDtypeStruct((t, d), _F32),
        compiler_params=pltpu.CompilerParams(
            dimension_semantics=("parallel",), vmem_limit_bytes=VMEM_LIMIT_BYTES),
        name="merge",
    )(x, y_four, y_rec, gates, proj_a, proj_b, w_out)


def kernel(x, ffn1_norm, ffn1_w_gate, ffn1_w_up, ffn1_w_down, mix_norm, w_in, b_gates, conv_w, conv_b, lru_wa, lru_ba, lru_wx, lru_bx, lru_lambda, proj_a, proj_b, w_out, ffn2_norm, ffn2_w_gate, ffn2_w_up, ffn2_w_down, final_norm):
    b, s, d = x.shape
    assert ffn1_norm.shape[0] == 1, "single-layer problem"
    bf = lambda w: w.astype(_BF16)
    x1, u, w_in_b = _ffn_call(
        x.reshape(b * s, d), ffn1_norm, bf(ffn1_w_gate[0]), bf(ffn1_w_up[0]), bf(ffn1_w_down[0]),
        mix_norm, emit_residual=True, casts=(w_in[0],), tm=512)
    z_four, z_rec, z_gelu, wg2, wu2, wd2, proj_a_b, proj_b_b, w_out_b = _inproj_call(
        u, w_in_b, casts=(ffn2_w_gate[0], ffn2_w_up[0], ffn2_w_down[0], proj_a[0], proj_b[0], w_out[0]))

    y_four = _fourier_call(z_four.reshape(b, s, FOURIER_WIDTH))

    wcat = jnp.concatenate([lru_wa[0, 0], lru_wx[0, 0], lru_wa[0, 1], lru_wx[0, 1]], axis=-1)
    bcat = 0.5 * jnp.concatenate([lru_ba[0, 0], lru_bx[0, 0], lru_ba[0, 1], lru_bx[0, 1]], axis=-1)
    lam = jnp.transpose(lru_lambda[0], (1, 0, 2))
    y_rec, gates = _lru_call(z_rec.reshape(b, s, LRU_WIDTH), z_gelu.reshape(b, s, LRU_WIDTH),
                             conv_w[0], conv_b, bf(wcat), bcat[:, None, :], lam,
                             u, w_in_b, b_gates.reshape(1, GATE_WIDTH))

    x2 = _merge_call(x1, y_four.reshape(b * s, FOURIER_WIDTH), y_rec.reshape(b * s, LRU_WIDTH), gates,
                     proj_a_b, proj_b_b, w_out_b)
    (out,) = _ffn_call(x2, ffn2_norm, wg2, wu2, wd2, final_norm[None], emit_residual=False, tm=512)
    return out.reshape(b, s, d)
```

```python
import functools
import math

import numpy as np
import jax
import jax.numpy as jnp
from jax import lax
from jax.experimental import pallas as pl
from jax.experimental.pallas import tpu as pltpu

D_MODEL = 2048
D_FF = 5632
FOURIER_WIDTH = 1024
FOURIER_GROUPS = 4
GROUP_DIM = FOURIER_WIDTH // FOURIER_GROUPS
LRU_WIDTH = 1024
LRU_HEADS = 8
HEAD_DIM = LRU_WIDTH // LRU_HEADS
LRU_C = 8.0
RMS_EPS = 1e-6
FFN_RES_SCALE = 0.5
GATE_WIDTH = 2 * D_MODEL
IN_WIDTH = FOURIER_WIDTH + 2 * LRU_WIDTH + GATE_WIDTH

VMEM_LIMIT_BYTES = 56 * 1024 * 1024
SUBLANES = 8
LANES = 128
CAST_ROWS = 2 * SUBLANES

FFT_BLOCKS = 16
SCAN_CHUNK = 128
SCAN_PITCH = SCAN_CHUNK + SUBLANES
CONV_PITCH = SCAN_CHUNK + 3 * SUBLANES
PIECE_ROWS = 64
GATE_STEPS = 8
GATE_PIECE = 256
SCAN_UNROLL = 4
GATE_LOOKAHEAD = 1
LOG2_E = 1.0 / math.log(2.0)
LN_2 = math.log(2.0)

_BF16 = jnp.bfloat16
_F32 = jnp.float32


def _dot(a, b):
    return jnp.dot(a, b, preferred_element_type=_F32)


def _sigmoid(x):
    return 0.5 * (1.0 + jnp.tanh(0.5 * x))


def _rms_norm(x, gain):
    ms = jnp.mean(x * x, axis=-1, keepdims=True)
    return x * lax.rsqrt(ms + RMS_EPS) * gain


def _gelu_tanh(x):
    return 0.5 * x * (1.0 + jnp.tanh(math.sqrt(2.0 / math.pi) * (x + 0.044715 * (x * x * x))))


def _ride_along_casts(weights, grid, first_row=0, n_rows=None):
    n_rows = grid[0] if n_rows is None else n_rows
    n_steps = n_rows * grid[1]
    specs, shapes = [], []
    for w in weights:
        rows, cols = w.shape
        block_rows = next(r for r in range(CAST_ROWS, rows + 1, CAST_ROWS)
                          if rows % r == 0 and rows // r <= n_steps)
        n_blocks = rows // block_rows

        def index_map(i, j, n_blocks=n_blocks):
            step = jnp.clip((i - first_row) * grid[1] + j, 0, n_steps - 1)
            return (step * n_blocks // n_steps, 0)

        specs.append(pl.BlockSpec((block_rows, cols), index_map))
        shapes.append(jax.ShapeDtypeStruct((rows, cols), _BF16))
    return specs, shapes


def _stage_lhs(scratch_ref, block_ref):
    scratch_ref[...] = block_ref[...]


def _cast_blocks(srcs, dsts):
    for src, dst in zip(srcs, dsts):
        dst[...] = src[...].astype(_BF16)


def _ffn_kernel(*refs, emit_residual, n_casts):
    x_ref, gain_ref, wg_ref, wu_ref, wd_ref, post_ref = refs[:6]
    cast_in = refs[6:6 + n_casts]
    outs = refs[6 + n_casts:]
    if emit_residual:
        res_ref, normed_ref = outs[:2]
        outs = outs[2:]
    else:
        normed_ref = outs[0]
        outs = outs[1:]
    cast_out = outs[:n_casts]
    (h_ref,) = outs[n_casts:]
    acc_ref = res_ref if emit_residual else normed_ref
    j = pl.program_id(1)

    @pl.when(j == 0)
    def _():
        h_ref[...] = _rms_norm(x_ref[...], gain_ref[...]).astype(_BF16)
        acc_ref[...] = jnp.zeros_like(acc_ref)

    h = h_ref[...]
    g = _dot(h, wg_ref[...])
    u = _dot(h, wu_ref[...])
    act = (g * _sigmoid(g) * u).astype(_BF16)
    acc_ref[...] += _dot(act, wd_ref[...])

    _cast_blocks(cast_in, cast_out)

    @pl.when(j == pl.num_programs(1) - 1)
    def _():
        y = x_ref[...] + FFN_RES_SCALE * acc_ref[...]
        if emit_residual:
            res_ref[...] = y
        normed_ref[...] = _rms_norm(y, post_ref[...]).astype(normed_ref.dtype)


def _ffn_call(x, gain, wg, wu, wd, post_gain, *, emit_residual, casts=(), tm, tf=512):
    t, d = x.shape
    f = wg.shape[1]
    grid = (t // tm, f // tf)
    row_spec = pl.BlockSpec((tm, d), lambda i, j: (i, 0))
    vec_spec = pl.BlockSpec((1, d), lambda i, j: (0, 0))
    in_specs = [
        row_spec,
        vec_spec,
        pl.BlockSpec((d, tf), lambda i, j: (0, j)),
        pl.BlockSpec((d, tf), lambda i, j: (0, j)),
        pl.BlockSpec((tf, d), lambda i, j: (j, 0)),
        vec_spec,
    ]
    out_shape = [jax.ShapeDtypeStruct((t, d), _BF16 if emit_residual else _F32)]
    out_specs = [row_spec]
    if emit_residual:
        out_shape.insert(0, jax.ShapeDtypeStruct((t, d), _F32))
        out_specs.insert(0, row_spec)
    cast_specs, cast_shapes = _ride_along_casts(casts, grid)
    in_specs += cast_specs
    out_specs += cast_specs
    out_shape += cast_shapes
    return pl.pallas_call(
        functools.partial(_ffn_kernel, emit_residual=emit_residual, n_casts=len(casts)),
        grid=grid,
        in_specs=in_specs,
        out_specs=tuple(out_specs),
        out_shape=tuple(out_shape),
        scratch_shapes=[pltpu.VMEM((tm, d), _BF16)],
        compiler_params=pltpu.CompilerParams(
            dimension_semantics=("arbitrary", "arbitrary"), vmem_limit_bytes=VMEM_LIMIT_BYTES),
        name="ffn_residual" if emit_residual else "ffn_final",
    )(x, gain, wg, wu, wd, post_gain, *casts)


def _inproj_kernel(*refs, n_casts):
    u_ref, w_ref = refs[:2]
    cast_in = refs[2:2 + n_casts]
    mixer_refs = refs[2 + n_casts:5 + n_casts]
    cast_out = refs[5 + n_casts:]
    j = pl.program_id(1)
    for col, ref in enumerate(mixer_refs):
        @pl.when(j == col)
        def _(ref=ref):
            ref[...] = _dot(u_ref[...], w_ref[...]).astype(_BF16)
            _cast_blocks(cast_in, cast_out)


def _inproj_call(u, w_in, *, casts=(), tm=1024, tn=1024):
    t, d = u.shape
    assert FOURIER_WIDTH == LRU_WIDTH == tn
    grid = (t // tm, 3)
    mixer_spec = pl.BlockSpec((tm, tn), lambda i, j: (i, 0))
    cast_specs, cast_shapes = _ride_along_casts(casts, grid)
    return pl.pallas_call(
        functools.partial(_inproj_kernel, n_casts=len(casts)),
        grid=grid,
        in_specs=[
            pl.BlockSpec((tm, d), lambda i, j: (i, 0)),
            pl.BlockSpec((d, tn), lambda i, j: (0, j)),
        ] + cast_specs,
        out_specs=(mixer_spec, mixer_spec, mixer_spec, *cast_specs),
        out_shape=(
            jax.ShapeDtypeStruct((t, FOURIER_WIDTH), _BF16),
            jax.ShapeDtypeStruct((t, LRU_WIDTH), _BF16),
            jax.ShapeDtypeStruct((t, LRU_WIDTH), _BF16),
            *cast_shapes,
        ),
        compiler_params=pltpu.CompilerParams(
            dimension_semantics=("arbitrary", "arbitrary"), vmem_limit_bytes=VMEM_LIMIT_BYTES),
        name="inproj",
    )(u, w_in, *casts)


def _fourier_constants(seq):
    rows = seq // FFT_BLOCKS
    c = np.arange(GROUP_DIM)
    ang_c = 2.0 * np.pi * np.outer(c, c) / GROUP_DIM
    chan = np.concatenate([np.cos(ang_c), -np.sin(ang_c)], axis=1) / math.sqrt(GROUP_DIM)
    n1 = np.arange(rows)
    k1 = np.arange(rows)
    mats = []
    for k2 in range(FFT_BLOCKS):
        ang = 2.0 * np.pi * np.outer(FFT_BLOCKS * k1 + k2, n1) / seq
        mats.append(np.concatenate([np.cos(ang), np.sin(ang)], axis=1) / math.sqrt(seq))
    seq_mats = np.stack(mats)
    tile = FFT_BLOCKS * FFT_BLOCKS
    rho = np.arange(tile)
    perm = np.zeros((tile, tile), np.float32)
    perm[rho, FFT_BLOCKS * (rho % FFT_BLOCKS) + rho // FFT_BLOCKS] = 1.0
    return (jnp.asarray(chan, _F32), jnp.asarray(seq_mats, _F32), jnp.asarray(perm, _F32))


def _fft_across_blocks(xs):
    n = len(xs)
    if n == 1:
        return xs
    ev = _fft_across_blocks(xs[0::2])
    od = _fft_across_blocks(xs[1::2])
    out = [None] * n
    for k in range(n // 2):
        o_r, o_i = od[k]
        e_r, e_i = ev[k]
        if k == 0:
            t_r, t_i = o_r, o_i
        elif 4 * k == n:
            out[k] = (e_r + o_i, e_i - o_r)
            out[k + n // 2] = (e_r - o_i, e_i + o_r)
            continue
        else:
            c = math.cos(2.0 * math.pi * k / n)
            s = -math.sin(2.0 * math.pi * k / n)
            t_r = c * o_r - s * o_i
            t_i = c * o_i + s * o_r
        out[k] = (e_r + t_r, e_i + t_i)
        out[k + n // 2] = (e_r - t_r, e_i - t_i)
    return out


def _fourier_kernel(z_ref, chan_ref, seq_ref, perm_ref, o_ref, w_ref, b_ref, g_ref):
    seq = z_ref.shape[1]
    rows = seq // FFT_BLOCKS
    gd = GROUP_DIM

    for blk in range(FFT_BLOCKS):
        sl = pl.ds(blk * rows, rows)
        w_ref[sl, :] = _dot(z_ref[0, sl, :], chan_ref[...])

    def butterfly(r, carry):
        r0 = pl.multiple_of(r * SUBLANES, SUBLANES)
        for lc in range(gd // LANES):
            re_l = pl.ds(lc * LANES, LANES)
            im_l = pl.ds(gd + lc * LANES, LANES)
            xs = [(w_ref[pl.ds(blk * rows + r0, SUBLANES), re_l],
                   w_ref[pl.ds(blk * rows + r0, SUBLANES), im_l]) for blk in range(FFT_BLOCKS)]
            ys = _fft_across_blocks(xs)
            for k2 in range(FFT_BLOCKS):
                b_ref[k2, pl.ds(r0, SUBLANES), pl.ds(lc * LANES, LANES)] = ys[k2][0]
                b_ref[k2, pl.ds(rows + r0, SUBLANES), pl.ds(lc * LANES, LANES)] = ys[k2][1]
        return carry

    lax.fori_loop(0, rows // SUBLANES, butterfly, 0, unroll=2)

    for k2 in range(FFT_BLOCKS):
        yk = _dot(seq_ref[k2], b_ref[k2].astype(_BF16)).astype(_BF16)
        for a in range(rows // FFT_BLOCKS):
            g_ref[a, pl.ds(FFT_BLOCKS * k2, FFT_BLOCKS), :] = yk[FFT_BLOCKS * a:FFT_BLOCKS * (a + 1), :]

    tile = FFT_BLOCKS * FFT_BLOCKS
    for a in range(rows // FFT_BLOCKS):
        o_ref[0, pl.ds(a * tile, tile), :] = _dot(perm_ref[...], g_ref[a]).astype(o_ref.dtype)


def _fourier_call(z_four):
    b, s, w = z_four.shape
    rows = s // FFT_BLOCKS
    assert rows == FFT_BLOCKS * FFT_BLOCKS and w % GROUP_DIM == 0
    chan, seq_mats, perm = (c.astype(_BF16) for c in _fourier_constants(s))
    blk = pl.BlockSpec((1, s, GROUP_DIM), lambda i, g: (i, 0, g))
    return pl.pallas_call(
        _fourier_kernel,
        grid=(b, w // GROUP_DIM),
        in_specs=[
            blk,
            pl.BlockSpec(chan.shape, lambda i, g: (0, 0)),
            pl.BlockSpec(seq_mats.shape, lambda i, g: (0, 0, 0)),
            pl.BlockSpec(perm.shape, lambda i, g: (0, 0)),
        ],
        out_specs=blk,
        out_shape=jax.ShapeDtypeStruct((b, s, w), _BF16),
        scratch_shapes=[
            pltpu.VMEM((s, 2 * GROUP_DIM), _F32),
            pltpu.VMEM((FFT_BLOCKS, 2 * rows, GROUP_DIM), _F32),
            pltpu.VMEM((rows // FFT_BLOCKS, FFT_BLOCKS * FFT_BLOCKS, GROUP_DIM), _BF16),
        ],
        compiler_params=pltpu.CompilerParams(
            dimension_semantics=("parallel", "parallel"), vmem_limit_bytes=VMEM_LIMIT_BYTES),
        name="fourier_mix",
    )(z_four, chan, seq_mats, perm)


def _softplus(x):
    return jnp.maximum(x, 0.0) + jnp.log1p(jnp.exp(-jnp.abs(x)))


def _lru_kernel(zr_ref, zg_ref, cw_ref, cb_ref, wcat_ref, bcat_ref, lam_ref, tok_ref, *refs):
    n_pieces = GATE_WIDTH // GATE_PIECE
    piece_w_refs = refs[:n_pieces]
    (gbias_ref, o_ref, gates_ref,
     zpad_ref, af_ref, uf_ref, ab_ref, ub_ref, hf_ref, hb_ref, tot_ref, cin_ref, lhs_ref) = refs[n_pieces:]
    _stage_lhs(lhs_ref, tok_ref)
    seq = zr_ref.shape[1]
    hd = HEAD_DIM
    n_chunks = seq // SCAN_CHUNK
    n_groups = n_chunks // SUBLANES
    halo = SUBLANES

    def fill(c, carry):
        r0 = pl.multiple_of(c * SCAN_CHUNK, SCAN_CHUNK)
        p0 = pl.multiple_of(c * CONV_PITCH, SUBLANES)
        zpad_ref[pl.ds(p0 + halo, SCAN_CHUNK), :] = zr_ref[0, pl.ds(r0, SCAN_CHUNK), :].astype(_F32)
        return carry

    lax.fori_loop(0, n_chunks, fill, 0)
    for c in range(n_chunks):
        before = (zpad_ref[pl.ds((c - 1) * CONV_PITCH + SCAN_CHUNK, halo), :] if c > 0
                  else jnp.zeros((halo, hd), _F32))
        after = (zpad_ref[pl.ds((c + 1) * CONV_PITCH + halo, halo), :] if c < n_chunks - 1
                 else jnp.zeros((halo, hd), _F32))
        zpad_ref[pl.ds(c * CONV_PITCH, halo), :] = before
        zpad_ref[pl.ds(c * CONV_PITCH + halo + SCAN_CHUNK, halo), :] = after

    cw = 0.5 * cw_ref[...]
    cb = 0.5 * cb_ref[...]
    half_rate = (-0.5 * LRU_C * LOG2_E) * _softplus(-lam_ref[0])
    directions = ((af_ref, uf_ref, hf_ref), (ab_ref, ub_ref, hb_ref))

    def gate_logits(tb):
        t0 = tb * GATE_STEPS
        taps = [zpad_ref[pl.ds(halo - 2 + t0 + m, n_chunks, stride=CONV_PITCH), :]
                for m in range(GATE_STEPS + 3)]
        hvs = [cb + sum(taps[j + k] * cw[k:k + 1, :] for k in range(4)) for j in range(GATE_STEPS)]
        return hvs, _dot(jnp.concatenate(hvs, axis=0).astype(_BF16), wcat_ref[0])

    def gates(tb, hvs, g_half):
        t0 = tb * GATE_STEPS
        for j, hv in enumerate(hvs):
            rows = pl.ds((t0 + j) * n_chunks, n_chunks)
            th = jnp.tanh(g_half[j * n_chunks:(j + 1) * n_chunks, :] + bcat_ref[0])
            for d, (a_ref, u_ref, _) in enumerate(directions):
                th_r = th[:, (2 * d) * hd:(2 * d + 1) * hd]
                th_i = th[:, (2 * d + 1) * hd:(2 * d + 2) * hd]
                hr = half_rate[d:d + 1, :]
                log2_a = hr + hr * th_r
                a = jnp.exp2(log2_a)
                x = jnp.tanh(log2_a * (-LN_2)) * (1.0 + a * a)
                scale = jnp.where(x > 0.0, x * lax.rsqrt(x), 0.0)
                a_ref[rows, :] = a
                u_ref[rows, :] = scale * (1.0 + th_i) * hv

    def gate_piece(n):
        cols = pl.ds(n * GATE_PIECE, GATE_PIECE)
        z = _dot(lhs_ref[...], piece_w_refs[n][...])
        for r in range(0, z.shape[0], PIECE_ROWS):
            gates_ref[pl.ds(r, PIECE_ROWS), cols] = _sigmoid(
                z[r:r + PIECE_ROWS, :] + gbias_ref[:, cols]).astype(gates_ref.dtype)

    n_blocks = SCAN_CHUNK // GATE_STEPS
    assert n_blocks % n_pieces == 0
    logits = [gate_logits(tb) for tb in range(GATE_LOOKAHEAD)]
    for tb in range(n_blocks):
        if tb + GATE_LOOKAHEAD < n_blocks:
            logits.append(gate_logits(tb + GATE_LOOKAHEAD))
        gates(tb, *logits.pop(0))
        if (tb + 1) % (n_blocks // n_pieces) == 0:
            gate_piece(tb // (n_blocks // n_pieces))

    def step_rows(d, g, i):
        t = i if d == 0 else SCAN_CHUNK - 1 - i
        return pl.ds(pl.multiple_of(t * n_chunks, n_chunks) + g * SUBLANES, SUBLANES)

    def chunk_rows(d, g, i):
        t = i if d == 0 else SCAN_CHUNK - 1 - i
        return pl.ds(g * SUBLANES * SCAN_PITCH + t, SUBLANES, stride=SCAN_PITCH)

    lanes = [(d, g) for d in range(2) for g in range(n_groups)]

    def totals(i, carry):
        out = []
        for (d, g), (prod, resp) in zip(lanes, carry):
            a = directions[d][0][step_rows(d, g, i), :]
            u = directions[d][1][step_rows(d, g, i), :]
            out.append((prod * a, a * resp + u))
        return tuple(out)

    init = tuple((jnp.ones((SUBLANES, hd), _F32), jnp.zeros((SUBLANES, hd), _F32)) for _ in lanes)
    tot = lax.fori_loop(0, SCAN_CHUNK, totals, init, unroll=SCAN_UNROLL)
    for (d, g), (prod, resp) in zip(lanes, tot):
        tot_ref[2 * d, pl.ds(g * SUBLANES, SUBLANES), :] = prod
        tot_ref[2 * d + 1, pl.ds(g * SUBLANES, SUBLANES), :] = resp

    for d in range(2):
        state = jnp.zeros((1, hd), _F32)
        for c in (range(n_chunks) if d == 0 else range(n_chunks - 1, -1, -1)):
            cin_ref[d, pl.ds(c, 1), :] = state
            state = tot_ref[2 * d, pl.ds(c, 1), :] * state + tot_ref[2 * d + 1, pl.ds(c, 1), :]

    def replay(i, carry):
        out = []
        for (d, g), h_in in zip(lanes, carry):
            a = directions[d][0][step_rows(d, g, i), :]
            u = directions[d][1][step_rows(d, g, i), :]
            h = a * h_in + u
            directions[d][2][chunk_rows(d, g, i), :] = h
            out.append(h)
        return tuple(out)

    lax.fori_loop(0, SCAN_CHUNK, replay,
                  tuple(cin_ref[d, pl.ds(g * SUBLANES, SUBLANES), :] for d, g in lanes), unroll=SCAN_UNROLL)

    def finish(c, carry):
        r0 = pl.multiple_of(c * SCAN_CHUNK, SCAN_CHUNK)
        p0 = pl.multiple_of(c * SCAN_PITCH, SUBLANES)
        h = hf_ref[pl.ds(p0, SCAN_CHUNK), :] + hb_ref[pl.ds(p0, SCAN_CHUNK), :]
        zg = zg_ref[0, pl.ds(r0, SCAN_CHUNK), :].astype(_F32)
        o_ref[0, pl.ds(r0, SCAN_CHUNK), :] = (h * _gelu_tanh(zg)).astype(o_ref.dtype)
        return carry

    lax.fori_loop(0, n_chunks, finish, 0)


def _lru_call(z_rec, z_gelu, conv_w, conv_b, wcat, bcat, lam, tokens, w_in, b_gates):
    b, s, w = z_rec.shape
    heads = w // HEAD_DIM
    n_chunks = s // SCAN_CHUNK
    assert s % (SCAN_CHUNK * SUBLANES) == 0
    t, d = tokens.shape
    tile = t // (b * heads)
    n_wrefs = GATE_WIDTH // GATE_PIECE
    wcols = GATE_PIECE
    first = (IN_WIDTH - GATE_WIDTH) // wcols
    assert (IN_WIDTH - GATE_WIDTH) % wcols == 0 and t % (b * heads) == 0
    blk = pl.BlockSpec((1, s, HEAD_DIM), lambda i, h: (i, 0, h))
    pitched = pltpu.VMEM((n_chunks * SCAN_PITCH, HEAD_DIM), _F32)
    time_major = pltpu.VMEM((s, HEAD_DIM), _F32)
    gate_w_specs = [
        pl.BlockSpec((d, wcols), lambda i, h, c=first + k: (0, c), pipeline_mode=pl.Buffered(1))
        for k in range(n_wrefs)]
    return pl.pallas_call(
        _lru_kernel,
        grid=(b, heads),
        in_specs=[
            blk,
            blk,
            pl.BlockSpec((conv_w.shape[0], HEAD_DIM), lambda i, h: (0, h)),
            pl.BlockSpec((1, HEAD_DIM), lambda i, h: (0, h)),
            pl.BlockSpec((1, HEAD_DIM, 4 * HEAD_DIM), lambda i, h: (h, 0, 0)),
            pl.BlockSpec((1, 1, 4 * HEAD_DIM), lambda i, h: (h, 0, 0)),
            pl.BlockSpec((1, 2, HEAD_DIM), lambda i, h: (h, 0, 0)),
            pl.BlockSpec((tile, d), lambda i, h: (i * heads + h, 0)),
            *gate_w_specs,
            pl.BlockSpec((1, GATE_WIDTH), lambda i, h: (0, 0)),
        ],
        out_specs=(blk, pl.BlockSpec((tile, GATE_WIDTH), lambda i, h: (i * heads + h, 0))),
        out_shape=(jax.ShapeDtypeStruct((b, s, w), _BF16), jax.ShapeDtypeStruct((t, GATE_WIDTH), _BF16)),
        scratch_shapes=[
            pltpu.VMEM((n_chunks * CONV_PITCH, HEAD_DIM), _F32),
            time_major, time_major, time_major, time_major, pitched, pitched,
            pltpu.VMEM((4, n_chunks, HEAD_DIM), _F32),
            pltpu.VMEM((2, n_chunks, HEAD_DIM), _F32),
            pltpu.VMEM((tile, d), _BF16),
        ],
        compiler_params=pltpu.CompilerParams(
            dimension_semantics=("parallel", "parallel"), vmem_limit_bytes=VMEM_LIMIT_BYTES),
        name="rglru_gates",
    )(z_rec, z_gelu, conv_w, conv_b, wcat, bcat, lam, tokens, *([w_in] * n_wrefs), b_gates)


def _merge_kernel(x_ref, yf_ref, yg_ref, gates_ref, pa_ref, pb_ref, wo_ref, o_ref):
    d = x_ref.shape[1]
    y_a = _dot(yf_ref[...], pa_ref[...])
    y_b = _dot(yg_ref[...], pb_ref[...])
    merged = gates_ref[:, :d].astype(_F32) * y_a + gates_ref[:, d:].astype(_F32) * y_b
    o_ref[...] = x_ref[...] + _dot(merged.astype(_BF16), wo_ref[...])


def _merge_call(x, y_four, y_rec, gates, proj_a, proj_b, w_out, *, tm=256):
    t, d = x.shape
    resident = functools.partial(pl.BlockSpec, index_map=lambda i: (0, 0), pipeline_mode=pl.Buffered(1))
    return pl.pallas_call(
        _merge_kernel,
        grid=(t // tm,),
        in_specs=[
            pl.BlockSpec((tm, d), lambda i: (i, 0)),
            pl.BlockSpec((tm, y_four.shape[1]), lambda i: (i, 0)),
            pl.BlockSpec((tm, y_rec.shape[1]), lambda i: (i, 0)),
            pl.BlockSpec((tm, gates.shape[1]), lambda i: (i, 0)),
            resident(proj_a.shape),
            resident(proj_b.shape),
            resident(w_out.shape),
        ],
        out_specs=pl.BlockSpec((tm, d), lambda i: (i, 0)),
        out_shape=jax.ShapeDtypeStruct((t, d), _F32),
        compiler_params=pltpu.CompilerParams(
            dimension_semantics=("parallel",), vmem_limit_bytes=VMEM_LIMIT_BYTES),
        name="merge",
    )(x, y_four, y_rec, gates, proj_a, proj_b, w_out)


def kernel(x, ffn1_norm, ffn1_w_gate, ffn1_w_up, ffn1_w_down, mix_norm, w_in, b_gates, conv_w, conv_b, lru_wa, lru_ba, lru_wx, lru_bx, lru_lambda, proj_a, proj_b, w_out, ffn2_norm, ffn2_w_gate, ffn2_w_up, ffn2_w_down, final_norm):
    b, s, d = x.shape
    assert ffn1_norm.shape[0] == 1, "single-layer problem"
    bf = lambda w: w.astype(_BF16)
    x1, u, w_in_b = _ffn_call(
        x.reshape(b * s, d), ffn1_norm, bf(ffn1_w_gate[0]), bf(ffn1_w_up[0]), bf(ffn1_w_down[0]),
        mix_norm, emit_residual=True, casts=(w_in[0],), tm=512)
    z_four, z_rec, z_gelu, wg2, wu2, wd2, proj_a_b, proj_b_b, w_out_b = _inproj_call(
        u, w_in_b, casts=(ffn2_w_gate[0], ffn2_w_up[0], ffn2_w_down[0], proj_a[0], proj_b[0], w_out[0]))

    y_four = _fourier_call(z_four.reshape(b, s, FOURIER_WIDTH))

    wcat = jnp.concatenate([lru_wa[0, 0], lru_wx[0, 0], lru_wa[0, 1], lru_wx[0, 1]], axis=-1)
    bcat = 0.5 * jnp.concatenate([lru_ba[0, 0], lru_bx[0, 0], lru_ba[0, 1], lru_bx[0, 1]], axis=-1)
    lam = jnp.transpose(lru_lambda[0], (1, 0, 2))
    y_rec, gates = _lru_call(z_rec.reshape(b, s, LRU_WIDTH), z_gelu.reshape(b, s, LRU_WIDTH),
                             conv_w[0], conv_b, bf(wcat), bcat[:, None, :], lam,
                             u, w_in_b, b_gates.reshape(1, GATE_WIDTH))

    x2 = _merge_call(x1, y_four.reshape(b * s, FOURIER_WIDTH), y_rec.reshape(b * s, LRU_WIDTH), gates,
                     proj_a_b, proj_b_b, w_out_b)
    (out,) = _ffn_call(x2, ffn2_norm, wg2, wu2, wd2, final_norm[None], emit_residual=False, tm=512)
    return out.reshape(b, s, d)
```

```python
import functools
import math

import numpy as np
import jax
import jax.numpy as jnp
from jax import lax
from jax.experimental import pallas as pl
from jax.experimental.pallas import tpu as pltpu

D_MODEL = 2048
D_FF = 5632
FOURIER_WIDTH = 1024
FOURIER_GROUPS = 4
GROUP_DIM = FOURIER_WIDTH // FOURIER_GROUPS
LRU_WIDTH = 1024
LRU_HEADS = 8
HEAD_DIM = LRU_WIDTH // LRU_HEADS
LRU_C = 8.0
RMS_EPS = 1e-6
FFN_RES_SCALE = 0.5
GATE_WIDTH = 2 * D_MODEL
IN_WIDTH = FOURIER_WIDTH + 2 * LRU_WIDTH + GATE_WIDTH

VMEM_LIMIT_BYTES = 56 * 1024 * 1024
SUBLANES = 8
LANES = 128
CAST_ROWS = 2 * SUBLANES

FFT_BLOCKS = 16
SCAN_CHUNK = 128
SCAN_PITCH = SCAN_CHUNK + SUBLANES
CONV_PITCH = SCAN_CHUNK + 3 * SUBLANES
PIECE_ROWS = 64
GATE_STEPS = 8
GATE_PIECE = 256
SCAN_UNROLL = 4
GATE_LOOKAHEAD = 1
LOG2_E = 1.0 / math.log(2.0)
LN_2 = math.log(2.0)

_BF16 = jnp.bfloat16
_F32 = jnp.float32


def _dot(a, b):
    return jnp.dot(a, b, preferred_element_type=_F32)


def _sigmoid(x):
    return 0.5 * (1.0 + jnp.tanh(0.5 * x))


def _rms_norm(x, gain):
    ms = jnp.mean(x * x, axis=-1, keepdims=True)
    return x * lax.rsqrt(ms + RMS_EPS) * gain


def _gelu_tanh(x):
    return 0.5 * x * (1.0 + jnp.tanh(math.sqrt(2.0 / math.pi) * (x + 0.044715 * (x * x * x))))


def _ride_along_casts(weights, grid):
    n_steps = math.prod(grid)
    specs, shapes = [], []
    for w in weights:
        rows, cols = w.shape
        block_rows = next(r for r in range(CAST_ROWS, rows + 1, CAST_ROWS)
                          if rows % r == 0 and rows // r <= n_steps)
        n_blocks = rows // block_rows

        def index_map(*idx, n_blocks=n_blocks):
            step = 0
            for i, extent in zip(idx, grid):
                step = step * extent + i
            return (step * n_blocks // n_steps, 0)

        specs.append(pl.BlockSpec((block_rows, cols), index_map))
        shapes.append(jax.ShapeDtypeStruct((rows, cols), _BF16))
    return specs, shapes


def _cast_blocks(srcs, dsts):
    for src, dst in zip(srcs, dsts):
        dst[...] = src[...].astype(_BF16)


def _ffn_kernel(*refs, emit_residual, n_casts):
    x_ref, gain_ref, wg_ref, wu_ref, wd_ref, post_ref = refs[:6]
    cast_in = refs[6:6 + n_casts]
    outs = refs[6 + n_casts:]
    if emit_residual:
        res_ref, normed_ref = outs[:2]
        outs = outs[2:]
    else:
        normed_ref = outs[0]
        outs = outs[1:]
    cast_out = outs[:n_casts]
    (h_ref,) = outs[n_casts:]
    acc_ref = res_ref if emit_residual else normed_ref
    j = pl.program_id(1)

    @pl.when(j == 0)
    def _():
        h_ref[...] = _rms_norm(x_ref[...], gain_ref[...]).astype(_BF16)
        acc_ref[...] = jnp.zeros_like(acc_ref)

    h = h_ref[...]
    g = _dot(h, wg_ref[...])
    u = _dot(h, wu_ref[...])
    act = (g * _sigmoid(g) * u).astype(_BF16)
    acc_ref[...] += _dot(act, wd_ref[...])

    _cast_blocks(cast_in, cast_out)

    @pl.when(j == pl.num_programs(1) - 1)
    def _():
        y = x_ref[...] + FFN_RES_SCALE * acc_ref[...]
        if emit_residual:
            res_ref[...] = y
        normed_ref[...] = _rms_norm(y, post_ref[...]).astype(normed_ref.dtype)


def _ffn_call(x, gain, wg, wu, wd, post_gain, *, emit_residual, casts=(), tm, tf=512):
    t, d = x.shape
    f = wg.shape[1]
    grid = (t // tm, f // tf)
    row_spec = pl.BlockSpec((tm, d), lambda i, j: (i, 0))
    vec_spec = pl.BlockSpec((1, d), lambda i, j: (0, 0))
    in_specs = [
        row_spec,
        vec_spec,
        pl.BlockSpec((d, tf), lambda i, j: (0, j)),
        pl.BlockSpec((d, tf), lambda i, j: (0, j)),
        pl.BlockSpec((tf, d), lambda i, j: (j, 0)),
        vec_spec,
    ]
    out_shape = [jax.ShapeDtypeStruct((t, d), _BF16 if emit_residual else _F32)]
    out_specs = [row_spec]
    if emit_residual:
        out_shape.insert(0, jax.ShapeDtypeStruct((t, d), _F32))
        out_specs.insert(0, row_spec)
    cast_specs, cast_shapes = _ride_along_casts(casts, grid)
    in_specs += cast_specs
    out_specs += cast_specs
    out_shape += cast_shapes
    return pl.pallas_call(
        functools.partial(_ffn_kernel, emit_residual=emit_residual, n_casts=len(casts)),
        grid=grid,
        in_specs=in_specs,
        out_specs=tuple(out_specs),
        out_shape=tuple(out_shape),
        scratch_shapes=[pltpu.VMEM((tm, d), _BF16)],
        compiler_params=pltpu.CompilerParams(
            dimension_semantics=("arbitrary", "arbitrary"), vmem_limit_bytes=VMEM_LIMIT_BYTES),
        name="ffn_residual" if emit_residual else "ffn_final",
    )(x, gain, wg, wu, wd, post_gain, *casts)


def _inproj_kernel(*refs, n_casts):
    u_ref, w_ref = refs[:2]
    cast_in = refs[2:2 + n_casts]
    mixer_refs = refs[2 + n_casts:5 + n_casts]
    cast_out = refs[5 + n_casts:]
    j = pl.program_id(1)
    for col, ref in enumerate(mixer_refs):
        @pl.when(j == col)
        def _(ref=ref):
            ref[...] = _dot(u_ref[...], w_ref[...]).astype(_BF16)
            _cast_blocks(cast_in, cast_out)


def _inproj_call(u, w_in, *, casts=(), tm=1024, tn=1024):
    t, d = u.shape
    assert FOURIER_WIDTH == LRU_WIDTH == tn
    grid = (t // tm, 3)
    mixer_spec = pl.BlockSpec((tm, tn), lambda i, j: (i, 0))
    cast_specs, cast_shapes = _ride_along_casts(casts, grid)
    return pl.pallas_call(
        functools.partial(_inproj_kernel, n_casts=len(casts)),
        grid=grid,
        in_specs=[
            pl.BlockSpec((tm, d), lambda i, j: (i, 0)),
            pl.BlockSpec((d, tn), lambda i, j: (0, j)),
        ] + cast_specs,
        out_specs=(mixer_spec, mixer_spec, mixer_spec, *cast_specs),
        out_shape=(
            jax.ShapeDtypeStruct((t, FOURIER_WIDTH), _BF16),
            jax.ShapeDtypeStruct((t, LRU_WIDTH), _BF16),
            jax.ShapeDtypeStruct((t, LRU_WIDTH), _BF16),
            *cast_shapes,
        ),
        compiler_params=pltpu.CompilerParams(
            dimension_semantics=("arbitrary", "arbitrary"), vmem_limit_bytes=VMEM_LIMIT_BYTES),
        name="inproj",
    )(u, w_in, *casts)


def _fourier_constants(seq):
    rows = seq // FFT_BLOCKS
    c = np.arange(GROUP_DIM)
    ang_c = 2.0 * np.pi * np.outer(c, c) / GROUP_DIM
    chan = np.concatenate([np.cos(ang_c), -np.sin(ang_c)], axis=1) / math.sqrt(GROUP_DIM)
    n1 = np.arange(rows)
    k1 = np.arange(rows)
    mats = []
    for k2 in range(FFT_BLOCKS):
        ang = 2.0 * np.pi * np.outer(FFT_BLOCKS * k1 + k2, n1) / seq
        mats.append(np.concatenate([np.cos(ang), np.sin(ang)], axis=1) / math.sqrt(seq))
    seq_mats = np.stack(mats)
    tile = FFT_BLOCKS * FFT_BLOCKS
    rho = np.arange(tile)
    perm = np.zeros((tile, tile), np.float32)
    perm[rho, FFT_BLOCKS * (rho % FFT_BLOCKS) + rho // FFT_BLOCKS] = 1.0
    return (jnp.asarray(chan, _F32), jnp.asarray(seq_mats, _F32), jnp.asarray(perm, _F32))


def _fft_across_blocks(xs):
    n = len(xs)
    if n == 1:
        return xs
    ev = _fft_across_blocks(xs[0::2])
    od = _fft_across_blocks(xs[1::2])
    out = [None] * n
    for k in range(n // 2):
        o_r, o_i = od[k]
        e_r, e_i = ev[k]
        if k == 0:
            t_r, t_i = o_r, o_i
        elif 4 * k == n:
            out[k] = (e_r + o_i, e_i - o_r)
            out[k + n // 2] = (e_r - o_i, e_i + o_r)
            continue
        else:
            c = math.cos(2.0 * math.pi * k / n)
            s = -math.sin(2.0 * math.pi * k / n)
            t_r = c * o_r - s * o_i
            t_i = c * o_i + s * o_r
        out[k] = (e_r + t_r, e_i + t_i)
        out[k + n // 2] = (e_r - t_r, e_i - t_i)
    return out


def _fourier_kernel(z_ref, chan_ref, seq_ref, perm_ref, *refs):
    n_casts = (len(refs) - 4) // 2
    cast_in, o_ref, cast_out = refs[:n_casts], refs[n_casts], refs[n_casts + 1:2 * n_casts + 1]
    w_ref, b_ref, g_ref = refs[2 * n_casts + 1:]
    _cast_blocks(cast_in, cast_out)
    seq = z_ref.shape[1]
    rows = seq // FFT_BLOCKS
    gd = GROUP_DIM

    for blk in range(FFT_BLOCKS):
        sl = pl.ds(blk * rows, rows)
        w_ref[sl, :] = _dot(z_ref[0, sl, :], chan_ref[...])

    def butterfly(r, carry):
        r0 = pl.multiple_of(r * SUBLANES, SUBLANES)
        for lc in range(gd // LANES):
            re_l = pl.ds(lc * LANES, LANES)
            im_l = pl.ds(gd + lc * LANES, LANES)
            xs = [(w_ref[pl.ds(blk * rows + r0, SUBLANES), re_l],
                   w_ref[pl.ds(blk * rows + r0, SUBLANES), im_l]) for blk in range(FFT_BLOCKS)]
            ys = _fft_across_blocks(xs)
            for k2 in range(FFT_BLOCKS):
                b_ref[k2, pl.ds(r0, SUBLANES), pl.ds(lc * LANES, LANES)] = ys[k2][0]
                b_ref[k2, pl.ds(rows + r0, SUBLANES), pl.ds(lc * LANES, LANES)] = ys[k2][1]
        return carry

    lax.fori_loop(0, rows // SUBLANES, butterfly, 0, unroll=2)

    for k2 in range(FFT_BLOCKS):
        yk = _dot(seq_ref[k2], b_ref[k2].astype(_BF16)).astype(_BF16)
        for a in range(rows // FFT_BLOCKS):
            g_ref[a, pl.ds(FFT_BLOCKS * k2, FFT_BLOCKS), :] = yk[FFT_BLOCKS * a:FFT_BLOCKS * (a + 1), :]

    tile = FFT_BLOCKS * FFT_BLOCKS
    for a in range(rows // FFT_BLOCKS):
        o_ref[0, pl.ds(a * tile, tile), :] = _dot(perm_ref[...], g_ref[a]).astype(o_ref.dtype)


def _fourier_call(z_four, *, casts=()):
    b, s, w = z_four.shape
    rows = s // FFT_BLOCKS
    assert rows == FFT_BLOCKS * FFT_BLOCKS and w % GROUP_DIM == 0
    chan, seq_mats, perm = (c.astype(_BF16) for c in _fourier_constants(s))
    blk = pl.BlockSpec((1, s, GROUP_DIM), lambda i, g: (i, 0, g))
    grid = (b, w // GROUP_DIM)
    cast_specs, cast_shapes = _ride_along_casts(casts, grid)
    return pl.pallas_call(
        _fourier_kernel,
        grid=grid,
        in_specs=[
            blk,
            pl.BlockSpec(chan.shape, lambda i, g: (0, 0)),
            pl.BlockSpec(seq_mats.shape, lambda i, g: (0, 0, 0)),
            pl.BlockSpec(perm.shape, lambda i, g: (0, 0)),
            *cast_specs,
        ],
        out_specs=(blk, *cast_specs),
        out_shape=(jax.ShapeDtypeStruct((b, s, w), _BF16), *cast_shapes),
        scratch_shapes=[
            pltpu.VMEM((s, 2 * GROUP_DIM), _F32),
            pltpu.VMEM((FFT_BLOCKS, 2 * rows, GROUP_DIM), _F32),
            pltpu.VMEM((rows // FFT_BLOCKS, FFT_BLOCKS * FFT_BLOCKS, GROUP_DIM), _BF16),
        ],
        compiler_params=pltpu.CompilerParams(
            dimension_semantics=("arbitrary", "arbitrary"), vmem_limit_bytes=VMEM_LIMIT_BYTES),
        name="fourier_mix",
    )(z_four, chan, seq_mats, perm, *casts)


def _softplus(x):
    return jnp.maximum(x, 0.0) + jnp.log1p(jnp.exp(-jnp.abs(x)))


def _lru_kernel(zr_ref, zg_ref, cw_ref, cb_ref, wcat_ref, bcat_ref, lam_ref,
                tok_ref, gw0_ref, gw1_ref, gw2_ref, gw3_ref, gbias_ref, o_ref, gates_ref,
                zpad_ref, af_ref, uf_ref, ab_ref, ub_ref, hf_ref, hb_ref, tot_ref, cin_ref):
    seq = zr_ref.shape[1]
    hd = HEAD_DIM
    n_chunks = seq // SCAN_CHUNK
    n_groups = n_chunks // SUBLANES
    halo = SUBLANES

    def fill(c, carry):
        r0 = pl.multiple_of(c * SCAN_CHUNK, SCAN_CHUNK)
        p0 = pl.multiple_of(c * CONV_PITCH, SUBLANES)
        zpad_ref[pl.ds(p0 + halo, SCAN_CHUNK), :] = zr_ref[0, pl.ds(r0, SCAN_CHUNK), :].astype(_F32)
        return carry

    lax.fori_loop(0, n_chunks, fill, 0)
    for c in range(n_chunks):
        before = (zpad_ref[pl.ds((c - 1) * CONV_PITCH + SCAN_CHUNK, halo), :] if c > 0
                  else jnp.zeros((halo, hd), _F32))
        after = (zpad_ref[pl.ds((c + 1) * CONV_PITCH + halo, halo), :] if c < n_chunks - 1
                 else jnp.zeros((halo, hd), _F32))
        zpad_ref[pl.ds(c * CONV_PITCH, halo), :] = before
        zpad_ref[pl.ds(c * CONV_PITCH + halo + SCAN_CHUNK, halo), :] = after

    cw = 0.5 * cw_ref[...]
    cb = 0.5 * cb_ref[...]
    half_rate = (-0.5 * LRU_C * LOG2_E) * _softplus(-lam_ref[0])
    directions = ((af_ref, uf_ref, hf_ref), (ab_ref, ub_ref, hb_ref))

    def gate_logits(tb):
        t0 = tb * GATE_STEPS
        taps = [zpad_ref[pl.ds(halo - 2 + t0 + m, n_chunks, stride=CONV_PITCH), :]
                for m in range(GATE_STEPS + 3)]
        hvs = [cb + sum(taps[j + k] * cw[k:k + 1, :] for k in range(4)) for j in range(GATE_STEPS)]
        return hvs, _dot(jnp.concatenate(hvs, axis=0).astype(_BF16), wcat_ref[0])

    def gates(tb, hvs, g_half):
        t0 = tb * GATE_STEPS
        for j, hv in enumerate(hvs):
            rows = pl.ds((t0 + j) * n_chunks, n_chunks)
            th = jnp.tanh(g_half[j * n_chunks:(j + 1) * n_chunks, :] + bcat_ref[0])
            for d, (a_ref, u_ref, _) in enumerate(directions):
                th_r = th[:, (2 * d) * hd:(2 * d + 1) * hd]
                th_i = th[:, (2 * d + 1) * hd:(2 * d + 2) * hd]
                hr = half_rate[d:d + 1, :]
                log2_a = hr + hr * th_r
                a = jnp.exp2(log2_a)
                x = jnp.tanh(log2_a * (-LN_2)) * (1.0 + a * a)
                scale = jnp.where(x > 0.0, x * lax.rsqrt(x), 0.0)
                a_ref[rows, :] = a
                u_ref[rows, :] = scale * (1.0 + th_i) * hv

    def gate_piece(n):
        w_ref = (gw0_ref, gw1_ref, gw2_ref, gw3_ref)[n // pieces_per_ref]
        c0 = (n % pieces_per_ref) * GATE_PIECE
        cols = pl.ds(n * GATE_PIECE, GATE_PIECE)
        z = _dot(tok_ref[...], w_ref[:, c0:c0 + GATE_PIECE])
        for r in range(0, z.shape[0], PIECE_ROWS):
            gates_ref[pl.ds(r, PIECE_ROWS), cols] = _sigmoid(
                z[r:r + PIECE_ROWS, :] + gbias_ref[:, cols]).astype(gates_ref.dtype)

    n_pieces = gates_ref.shape[1] // GATE_PIECE
    pieces_per_ref = gw0_ref.shape[1] // GATE_PIECE
    n_blocks = SCAN_CHUNK // GATE_STEPS
    assert n_blocks % n_pieces == 0
    logits = [gate_logits(tb) for tb in range(GATE_LOOKAHEAD)]
    for tb in range(n_blocks):
        if tb + GATE_LOOKAHEAD < n_blocks:
            logits.append(gate_logits(tb + GATE_LOOKAHEAD))
        gates(tb, *logits.pop(0))
        if (tb + 1) % (n_blocks // n_pieces) == 0:
            gate_piece(tb // (n_blocks // n_pieces))

    def step_rows(d, g, i):
        t = i if d == 0 else SCAN_CHUNK - 1 - i
        return pl.ds(pl.multiple_of(t * n_chunks, n_chunks) + g * SUBLANES, SUBLANES)

    def chunk_rows(d, g, i):
        t = i if d == 0 else SCAN_CHUNK - 1 - i
        return pl.ds(g * SUBLANES * SCAN_PITCH + t, SUBLANES, stride=SCAN_PITCH)

    lanes = [(d, g) for d in range(2) for g in range(n_groups)]

    def totals(i, carry):
        out = []
        for (d, g), (prod, resp) in zip(lanes, carry):
            a = directions[d][0][step_rows(d, g, i), :]
            u = directions[d][1][step_rows(d, g, i), :]
            out.append((prod * a, a * resp + u))
        return tuple(out)

    init = tuple((jnp.ones((SUBLANES, hd), _F32), jnp.zeros((SUBLANES, hd), _F32)) for _ in lanes)
    tot = lax.fori_loop(0, SCAN_CHUNK, totals, init, unroll=SCAN_UNROLL)
    for (d, g), (prod, resp) in zip(lanes, tot):
        tot_ref[2 * d, pl.ds(g * SUBLANES, SUBLANES), :] = prod
        tot_ref[2 * d + 1, pl.ds(g * SUBLANES, SUBLANES), :] = resp

    for d in range(2):
        state = jnp.zeros((1, hd), _F32)
        for c in (range(n_chunks) if d == 0 else range(n_chunks - 1, -1, -1)):
            cin_ref[d, pl.ds(c, 1), :] = state
            state = tot_ref[2 * d, pl.ds(c, 1), :] * state + tot_ref[2 * d + 1, pl.ds(c, 1), :]

    def replay(i, carry):
        out = []
        for (d, g), h_in in zip(lanes, carry):
            a = directions[d][0][step_rows(d, g, i), :]
            u = directions[d][1][step_rows(d, g, i), :]
            h = a * h_in + u
            directions[d][2][chunk_rows(d, g, i), :] = h
            out.append(h)
        return tuple(out)

    lax.fori_loop(0, SCAN_CHUNK, replay,
                  tuple(cin_ref[d, pl.ds(g * SUBLANES, SUBLANES), :] for d, g in lanes), unroll=SCAN_UNROLL)

    def finish(c, carry):
        r0 = pl.multiple_of(c * SCAN_CHUNK, SCAN_CHUNK)
        p0 = pl.multiple_of(c * SCAN_PITCH, SUBLANES)
        h = hf_ref[pl.ds(p0, SCAN_CHUNK), :] + hb_ref[pl.ds(p0, SCAN_CHUNK), :]
        zg = zg_ref[0, pl.ds(r0, SCAN_CHUNK), :].astype(_F32)
        o_ref[0, pl.ds(r0, SCAN_CHUNK), :] = (h * _gelu_tanh(zg)).astype(o_ref.dtype)
        return carry

    lax.fori_loop(0, n_chunks, finish, 0)


def _lru_call(z_rec, z_gelu, conv_w, conv_b, wcat, bcat, lam, tokens, w_in, b_gates):
    b, s, w = z_rec.shape
    heads = w // HEAD_DIM
    n_chunks = s // SCAN_CHUNK
    assert s % (SCAN_CHUNK * SUBLANES) == 0
    t, d = tokens.shape
    tile = t // (b * heads)
    n_wrefs = 4
    wcols = GATE_WIDTH // n_wrefs
    first = (IN_WIDTH - GATE_WIDTH) // wcols
    assert (IN_WIDTH - GATE_WIDTH) % wcols == 0 and t % (b * heads) == 0
    blk = pl.BlockSpec((1, s, HEAD_DIM), lambda i, h: (i, 0, h))
    pitched = pltpu.VMEM((n_chunks * SCAN_PITCH, HEAD_DIM), _F32)
    time_major = pltpu.VMEM((s, HEAD_DIM), _F32)
    gate_w_specs = [
        pl.BlockSpec((d, wcols), lambda i, h, c=first + k: (0, c), pipeline_mode=pl.Buffered(1))
        for k in range(n_wrefs)]
    return pl.pallas_call(
        _lru_kernel,
        grid=(b, heads),
        in_specs=[
            blk,
            blk,
            pl.BlockSpec((conv_w.shape[0], HEAD_DIM), lambda i, h: (0, h)),
            pl.BlockSpec((1, HEAD_DIM), lambda i, h: (0, h)),
            pl.BlockSpec((1, HEAD_DIM, 4 * HEAD_DIM), lambda i, h: (h, 0, 0)),
            pl.BlockSpec((1, 1, 4 * HEAD_DIM), lambda i, h: (h, 0, 0)),
            pl.BlockSpec((1, 2, HEAD_DIM), lambda i, h: (h, 0, 0)),
            pl.BlockSpec((tile, d), lambda i, h: (i * heads + h, 0)),
            *gate_w_specs,
            pl.BlockSpec((1, GATE_WIDTH), lambda i, h: (0, 0)),
        ],
        out_specs=(blk, pl.BlockSpec((tile, GATE_WIDTH), lambda i, h: (i * heads + h, 0))),
        out_shape=(jax.ShapeDtypeStruct((b, s, w), _BF16), jax.ShapeDtypeStruct((t, GATE_WIDTH), _BF16)),
        scratch_shapes=[
            pltpu.VMEM((n_chunks * CONV_PITCH, HEAD_DIM), _F32),
            time_major, time_major, time_major, time_major, pitched, pitched,
            pltpu.VMEM((4, n_chunks, HEAD_DIM), _F32),
            pltpu.VMEM((2, n_chunks, HEAD_DIM), _F32),
        ],
        compiler_params=pltpu.CompilerParams(
            dimension_semantics=("parallel", "parallel"), vmem_limit_bytes=VMEM_LIMIT_BYTES),
        name="rglru_gates",
    )(z_rec, z_gelu, conv_w, conv_b, wcat, bcat, lam, tokens, w_in, w_in, w_in, w_in, b_gates)


def _merge_kernel(x_ref, yf_ref, yg_ref, gates_ref, pa_ref, pb_ref, wo_ref, *refs):
    n_casts = (len(refs) - 1) // 2
    cast_in, o_ref, cast_out = refs[:n_casts], refs[n_casts], refs[n_casts + 1:]
    d = x_ref.shape[1]
    y_a = _dot(yf_ref[...], pa_ref[...])
    y_b = _dot(yg_ref[...], pb_ref[...])
    merged = gates_ref[:, :d].astype(_F32) * y_a + gates_ref[:, d:].astype(_F32) * y_b
    o_ref[...] = x_ref[...] + _dot(merged.astype(_BF16), wo_ref[...])
    _cast_blocks(cast_in, cast_out)


def _merge_call(x, y_four, y_rec, gates, proj_a, proj_b, w_out, *, casts=(), tm=256):
    t, d = x.shape
    grid = (t // tm,)
    resident = functools.partial(pl.BlockSpec, index_map=lambda i: (0, 0), pipeline_mode=pl.Buffered(1))
    row_spec = pl.BlockSpec((tm, d), lambda i: (i, 0))
    cast_specs, cast_shapes = _ride_along_casts(casts, grid)
    return pl.pallas_call(
        _merge_kernel,
        grid=grid,
        in_specs=[
            row_spec,
            pl.BlockSpec((tm, y_four.shape[1]), lambda i: (i, 0)),
            pl.BlockSpec((tm, y_rec.shape[1]), lambda i: (i, 0)),
            pl.BlockSpec((tm, gates.shape[1]), lambda i: (i, 0)),
            resident(proj_a.shape),
            resident(proj_b.shape),
            resident(w_out.shape),
            *cast_specs,
        ],
        out_specs=(row_spec, *cast_specs),
        out_shape=(jax.ShapeDtypeStruct((t, d), _F32), *cast_shapes),
        compiler_params=pltpu.CompilerParams(
            dimension_semantics=("arbitrary",), vmem_limit_bytes=VMEM_LIMIT_BYTES),
        name="merge",
    )(x, y_four, y_rec, gates, proj_a, proj_b, w_out, *casts)


def kernel(x, ffn1_norm, ffn1_w_gate, ffn1_w_up, ffn1_w_down, mix_norm, w_in, b_gates, conv_w, conv_b, lru_wa, lru_ba, lru_wx, lru_bx, lru_lambda, proj_a, proj_b, w_out, ffn2_norm, ffn2_w_gate, ffn2_w_up, ffn2_w_down, final_norm):
    b, s, d = x.shape
    assert ffn1_norm.shape[0] == 1, "single-layer problem"
    bf = lambda w: w.astype(_BF16)
    x1, u, w_in_b = _ffn_call(
        x.reshape(b * s, d), ffn1_norm, bf(ffn1_w_gate[0]), bf(ffn1_w_up[0]), bf(ffn1_w_down[0]),
        mix_norm, emit_residual=True, casts=(w_in[0],), tm=512)
    z_four, z_rec, z_gelu, wg2, proj_a_b, proj_b_b, w_out_b = _inproj_call(
        u, w_in_b, casts=(ffn2_w_gate[0], proj_a[0], proj_b[0], w_out[0]))

    y_four, wd2 = _fourier_call(z_four.reshape(b, s, FOURIER_WIDTH), casts=(ffn2_w_down[0],))

    wcat = jnp.concatenate([lru_wa[0, 0], lru_wx[0, 0], lru_wa[0, 1], lru_wx[0, 1]], axis=-1)
    bcat = 0.5 * jnp.concatenate([lru_ba[0, 0], lru_bx[0, 0], lru_ba[0, 1], lru_bx[0, 1]], axis=-1)
    lam = jnp.transpose(lru_lambda[0], (1, 0, 2))
    y_rec, gates = _lru_call(z_rec.reshape(b, s, LRU_WIDTH), z_gelu.reshape(b, s, LRU_WIDTH),
                             conv_w[0], conv_b, bf(wcat), bcat[:, None, :], lam,
                             u, w_in_b, b_gates.reshape(1, GATE_WIDTH))

    x2, wu2 = _merge_call(x1, y_four.reshape(b * s, FOURIER_WIDTH), y_rec.reshape(b * s, LRU_WIDTH), gates,
                          proj_a_b, proj_b_b, w_out_b, casts=(ffn2_w_up[0],))
    (out,) = _ffn_call(x2, ffn2_norm, wg2, wu2, wd2, final_norm[None], emit_residual=False, tm=512)
    return out.reshape(b, s, d)
```

```python
import functools
import math

import numpy as np
import jax
import jax.numpy as jnp
from jax import lax
from jax.experimental import pallas as pl
from jax.experimental.pallas import tpu as pltpu

D_MODEL = 2048
D_FF = 5632
FOURIER_WIDTH = 1024
FOURIER_GROUPS = 4
GROUP_DIM = FOURIER_WIDTH // FOURIER_GROUPS
LRU_WIDTH = 1024
LRU_HEADS = 8
HEAD_DIM = LRU_WIDTH // LRU_HEADS
LRU_C = 8.0
RMS_EPS = 1e-6
FFN_RES_SCALE = 0.5
GATE_WIDTH = 2 * D_MODEL
IN_WIDTH = FOURIER_WIDTH + 2 * LRU_WIDTH + GATE_WIDTH

VMEM_LIMIT_BYTES = 56 * 1024 * 1024
SUBLANES = 8
LANES = 128
CAST_ROWS = 2 * SUBLANES

FFT_BLOCKS = 16
SCAN_CHUNK = 128
SCAN_PITCH = SCAN_CHUNK + SUBLANES
CONV_PITCH = SCAN_CHUNK + 3 * SUBLANES
PIECE_ROWS = 64
GATE_STEPS = 8
GATE_PIECE = 256
SCAN_UNROLL = 4
GATE_LOOKAHEAD = 1
LOG2_E = 1.0 / math.log(2.0)
LN_2 = math.log(2.0)

_BF16 = jnp.bfloat16
_F32 = jnp.float32


def _dot(a, b):
    return jnp.dot(a, b, preferred_element_type=_F32)


def _sigmoid(x):
    return 0.5 * (1.0 + jnp.tanh(0.5 * x))


def _rms_norm(x, gain):
    ms = jnp.mean(x * x, axis=-1, keepdims=True)
    return x * lax.rsqrt(ms + RMS_EPS) * gain


def _gelu_tanh(x):
    return 0.5 * x * (1.0 + jnp.tanh(math.sqrt(2.0 / math.pi) * (x + 0.044715 * (x * x * x))))


def _ride_along_casts(weights, grid):
    n_steps = math.prod(grid)
    specs, shapes = [], []
    for w in weights:
        rows, cols = w.shape
        block_rows = next(r for r in range(CAST_ROWS, rows + 1, CAST_ROWS)
                          if rows % r == 0 and rows // r <= n_steps)
        n_blocks = rows // block_rows

        def index_map(*idx, n_blocks=n_blocks):
            step = 0
            for i, extent in zip(idx, grid):
                step = step * extent + i
            return (step * n_blocks // n_steps, 0)

        specs.append(pl.BlockSpec((block_rows, cols), index_map))
        shapes.append(jax.ShapeDtypeStruct((rows, cols), _BF16))
    return specs, shapes


def _cast_blocks(srcs, dsts):
    for src, dst in zip(srcs, dsts):
        dst[...] = src[...].astype(_BF16)


def _ffn_kernel(*refs, emit_residual, n_casts):
    x_ref, gain_ref, wg_ref, wu_ref, wd_ref, post_ref = refs[:6]
    cast_in = refs[6:6 + n_casts]
    outs = refs[6 + n_casts:]
    if emit_residual:
        res_ref, normed_ref = outs[:2]
        outs = outs[2:]
    else:
        normed_ref = outs[0]
        outs = outs[1:]
    cast_out = outs[:n_casts]
    (h_ref,) = outs[n_casts:]
    acc_ref = res_ref if emit_residual else normed_ref
    j = pl.program_id(1)

    @pl.when(j == 0)
    def _():
        x = x_ref[...]
        h_ref[...] = _rms_norm(x, gain_ref[...]).astype(_BF16)
        acc_ref[...] = (1.0 / FFN_RES_SCALE) * x

    h = h_ref[...]
    g = _dot(h, wg_ref[...])
    u = _dot(h, wu_ref[...])
    act = (g * _sigmoid(g) * u).astype(_BF16)
    acc_ref[...] += _dot(act, wd_ref[...])

    _cast_blocks(cast_in, cast_out)

    @pl.when(j == pl.num_programs(1) - 1)
    def _():
        y = FFN_RES_SCALE * acc_ref[...]
        if emit_residual:
            res_ref[...] = y
        normed_ref[...] = _rms_norm(y, post_ref[...]).astype(normed_ref.dtype)


def _ffn_call(x, gain, wg, wu, wd, post_gain, *, emit_residual, casts=(), tm, tf=512):
    t, d = x.shape
    f = wg.shape[1]
    grid = (t // tm, f // tf)
    row_spec = pl.BlockSpec((tm, d), lambda i, j: (i, 0))
    vec_spec = pl.BlockSpec((1, d), lambda i, j: (0, 0))
    x_spec = pl.BlockSpec(
        (tm, d), lambda i, j: (jnp.minimum(i + (j >= grid[1] // 2).astype(jnp.int32), grid[0] - 1), 0))
    in_specs = [
        x_spec,
        vec_spec,
        pl.BlockSpec((d, tf), lambda i, j: (0, j)),
        pl.BlockSpec((d, tf), lambda i, j: (0, j)),
        pl.BlockSpec((tf, d), lambda i, j: (j, 0)),
        vec_spec,
    ]
    out_shape = [jax.ShapeDtypeStruct((t, d), _BF16 if emit_residual else _F32)]
    out_specs = [row_spec]
    if emit_residual:
        out_shape.insert(0, jax.ShapeDtypeStruct((t, d), _F32))
        out_specs.insert(0, row_spec)
    cast_specs, cast_shapes = _ride_along_casts(casts, grid)
    in_specs += cast_specs
    out_specs += cast_specs
    out_shape += cast_shapes
    return pl.pallas_call(
        functools.partial(_ffn_kernel, emit_residual=emit_residual, n_casts=len(casts)),
        grid=grid,
        in_specs=in_specs,
        out_specs=tuple(out_specs),
        out_shape=tuple(out_shape),
        scratch_shapes=[pltpu.VMEM((tm, d), _BF16)],
        compiler_params=pltpu.CompilerParams(
            dimension_semantics=("arbitrary", "arbitrary"), vmem_limit_bytes=VMEM_LIMIT_BYTES),
        name="ffn_residual" if emit_residual else "ffn_final",
    )(x, gain, wg, wu, wd, post_gain, *casts)


def _inproj_kernel(*refs, n_casts):
    u_ref, w_ref = refs[:2]
    cast_in = refs[2:2 + n_casts]
    mixer_refs = refs[2 + n_casts:5 + n_casts]
    cast_out = refs[5 + n_casts:]
    j = pl.program_id(1)
    for col, ref in enumerate(mixer_refs):
        @pl.when(j == col)
        def _(ref=ref):
            ref[...] = _dot(u_ref[...], w_ref[...]).astype(_BF16)
            _cast_blocks(cast_in, cast_out)


def _inproj_call(u, w_in, *, casts=(), tm=1024, tn=1024):
    t, d = u.shape
    assert FOURIER_WIDTH == LRU_WIDTH == tn
    grid = (t // tm, 3)
    mixer_spec = pl.BlockSpec((tm, tn), lambda i, j: (i, 0))
    cast_specs, cast_shapes = _ride_along_casts(casts, grid)
    return pl.pallas_call(
        functools.partial(_inproj_kernel, n_casts=len(casts)),
        grid=grid,
        in_specs=[
            pl.BlockSpec((tm, d), lambda i, j: (i, 0)),
            pl.BlockSpec((d, tn), lambda i, j: (0, j)),
        ] + cast_specs,
        out_specs=(mixer_spec, mixer_spec, mixer_spec, *cast_specs),
        out_shape=(
            jax.ShapeDtypeStruct((t, FOURIER_WIDTH), _BF16),
            jax.ShapeDtypeStruct((t, LRU_WIDTH), _BF16),
            jax.ShapeDtypeStruct((t, LRU_WIDTH), _BF16),
            *cast_shapes,
        ),
        compiler_params=pltpu.CompilerParams(
            dimension_semantics=("arbitrary", "arbitrary"), vmem_limit_bytes=VMEM_LIMIT_BYTES),
        name="inproj",
    )(u, w_in, *casts)


def _fourier_constants(seq):
    rows = seq // FFT_BLOCKS
    c = np.arange(GROUP_DIM)
    ang_c = 2.0 * np.pi * np.outer(c, c) / GROUP_DIM
    chan = np.concatenate([np.cos(ang_c), -np.sin(ang_c)], axis=1) / math.sqrt(GROUP_DIM)
    n1 = np.arange(rows)
    k1 = np.arange(rows)
    mats = []
    for k2 in range(FFT_BLOCKS):
        ang = 2.0 * np.pi * np.outer(FFT_BLOCKS * k1 + k2, n1) / seq
        mats.append(np.concatenate([np.cos(ang), np.sin(ang)], axis=1) / math.sqrt(seq))
    seq_mats = np.stack(mats)
    tile = FFT_BLOCKS * FFT_BLOCKS
    rho = np.arange(tile)
    perm = np.zeros((tile, tile), np.float32)
    perm[rho, FFT_BLOCKS * (rho % FFT_BLOCKS) + rho // FFT_BLOCKS] = 1.0
    return (jnp.asarray(chan, _F32), jnp.asarray(seq_mats, _F32), jnp.asarray(perm, _F32))


def _fft_across_blocks(xs):
    n = len(xs)
    if n == 1:
        return xs
    ev = _fft_across_blocks(xs[0::2])
    od = _fft_across_blocks(xs[1::2])
    out = [None] * n
    for k in range(n // 2):
        o_r, o_i = od[k]
        e_r, e_i = ev[k]
        if k == 0:
            t_r, t_i = o_r, o_i
        elif 4 * k == n:
            out[k] = (e_r + o_i, e_i - o_r)
            out[k + n // 2] = (e_r - o_i, e_i + o_r)
            continue
        else:
            c = math.cos(2.0 * math.pi * k / n)
            s = -math.sin(2.0 * math.pi * k / n)
            t_r = c * o_r - s * o_i
            t_i = c * o_i + s * o_r
        out[k] = (e_r + t_r, e_i + t_i)
        out[k + n // 2] = (e_r - t_r, e_i - t_i)
    return out


def _fourier_kernel(z_ref, chan_ref, seq_ref, perm_ref, *refs):
    n_casts = (len(refs) - 4) // 2
    cast_in, o_ref, cast_out = refs[:n_casts], refs[n_casts], refs[n_casts + 1:2 * n_casts + 1]
    w_ref, b_ref, g_ref = refs[2 * n_casts + 1:]
    _cast_blocks(cast_in, cast_out)
    seq = z_ref.shape[1]
    rows = seq // FFT_BLOCKS
    gd = GROUP_DIM

    for blk in range(FFT_BLOCKS):
        sl = pl.ds(blk * rows, rows)
        w_ref[sl, :] = _dot(z_ref[0, sl, :], chan_ref[...])

    def butterfly(r, carry):
        r0 = pl.multiple_of(r * SUBLANES, SUBLANES)
        for lc in range(gd // LANES):
            re_l = pl.ds(lc * LANES, LANES)
            im_l = pl.ds(gd + lc * LANES, LANES)
            xs = [(w_ref[pl.ds(blk * rows + r0, SUBLANES), re_l],
                   w_ref[pl.ds(blk * rows + r0, SUBLANES), im_l]) for blk in range(FFT_BLOCKS)]
            ys = _fft_across_blocks(xs)
            for k2 in range(FFT_BLOCKS):
                b_ref[k2, pl.ds(r0, SUBLANES), pl.ds(lc * LANES, LANES)] = ys[k2][0]
                b_ref[k2, pl.ds(rows + r0, SUBLANES), pl.ds(lc * LANES, LANES)] = ys[k2][1]
        return carry

    lax.fori_loop(0, rows // SUBLANES, butterfly, 0, unroll=2)

    for k2 in range(FFT_BLOCKS):
        yk = _dot(seq_ref[k2], b_ref[k2].astype(_BF16)).astype(_BF16)
        for a in range(rows // FFT_BLOCKS):
            g_ref[a, pl.ds(FFT_BLOCKS * k2, FFT_BLOCKS), :] = yk[FFT_BLOCKS * a:FFT_BLOCKS * (a + 1), :]

    tile = FFT_BLOCKS * FFT_BLOCKS
    for a in range(rows // FFT_BLOCKS):
        o_ref[0, pl.ds(a * tile, tile), :] = _dot(perm_ref[...], g_ref[a]).astype(o_ref.dtype)


def _fourier_call(z_four, *, casts=()):
    b, s, w = z_four.shape
    rows = s // FFT_BLOCKS
    assert rows == FFT_BLOCKS * FFT_BLOCKS and w % GROUP_DIM == 0
    chan, seq_mats, perm = (c.astype(_BF16) for c in _fourier_constants(s))
    blk = pl.BlockSpec((1, s, GROUP_DIM), lambda i, g: (i, 0, g))
    grid = (b, w // GROUP_DIM)
    cast_specs, cast_shapes = _ride_along_casts(casts, grid)
    return pl.pallas_call(
        _fourier_kernel,
        grid=grid,
        in_specs=[
            blk,
            pl.BlockSpec(chan.shape, lambda i, g: (0, 0)),
            pl.BlockSpec(seq_mats.shape, lambda i, g: (0, 0, 0)),
            pl.BlockSpec(perm.shape, lambda i, g: (0, 0)),
            *cast_specs,
        ],
        out_specs=(blk, *cast_specs),
        out_shape=(jax.ShapeDtypeStruct((b, s, w), _BF16), *cast_shapes),
        scratch_shapes=[
            pltpu.VMEM((s, 2 * GROUP_DIM), _F32),
            pltpu.VMEM((FFT_BLOCKS, 2 * rows, GROUP_DIM), _F32),
            pltpu.VMEM((rows // FFT_BLOCKS, FFT_BLOCKS * FFT_BLOCKS, GROUP_DIM), _BF16),
        ],
        compiler_params=pltpu.CompilerParams(
            dimension_semantics=("arbitrary", "arbitrary"), vmem_limit_bytes=VMEM_LIMIT_BYTES),
        name="fourier_mix",
    )(z_four, chan, seq_mats, perm, *casts)


def _softplus(x):
    return jnp.maximum(x, 0.0) + jnp.log1p(jnp.exp(-jnp.abs(x)))


def _lru_kernel(zr_ref, zg_ref, cw_ref, cb_ref, wcat_ref, bcat_ref, lam_ref,
                tok_ref, gw0_ref, gw1_ref, gw2_ref, gw3_ref, gbias_ref, o_ref, gates_ref,
                zpad_ref, af_ref, uf_ref, ab_ref, ub_ref, hf_ref, hb_ref, tot_ref, cin_ref):
    seq = zr_ref.shape[1]
    hd = HEAD_DIM
    n_chunks = seq // SCAN_CHUNK
    n_groups = n_chunks // SUBLANES
    halo = SUBLANES

    def fill(c, carry):
        r0 = pl.multiple_of(c * SCAN_CHUNK, SCAN_CHUNK)
        p0 = pl.multiple_of(c * CONV_PITCH, SUBLANES)
        zpad_ref[pl.ds(p0 + halo, SCAN_CHUNK), :] = zr_ref[0, pl.ds(r0, SCAN_CHUNK), :].astype(_F32)
        return carry

    lax.fori_loop(0, n_chunks, fill, 0)
    for c in range(n_chunks):
        before = (zpad_ref[pl.ds((c - 1) * CONV_PITCH + SCAN_CHUNK, halo), :] if c > 0
                  else jnp.zeros((halo, hd), _F32))
        after = (zpad_ref[pl.ds((c + 1) * CONV_PITCH + halo, halo), :] if c < n_chunks - 1
                 else jnp.zeros((halo, hd), _F32))
        zpad_ref[pl.ds(c * CONV_PITCH, halo), :] = before
        zpad_ref[pl.ds(c * CONV_PITCH + halo + SCAN_CHUNK, halo), :] = after

    cw = 0.5 * cw_ref[...]
    cb = 0.5 * cb_ref[...]
    half_rate = (-0.5 * LRU_C * LOG2_E) * _softplus(-lam_ref[0])
    directions = ((af_ref, uf_ref, hf_ref), (ab_ref, ub_ref, hb_ref))

    def gate_logits(tb):
        t0 = tb * GATE_STEPS
        taps = [zpad_ref[pl.ds(halo - 2 + t0 + m, n_chunks, stride=CONV_PITCH), :]
                for m in range(GATE_STEPS + 3)]
        hvs = [cb + sum(taps[j + k] * cw[k:k + 1, :] for k in range(4)) for j in range(GATE_STEPS)]
        return hvs, _dot(jnp.concatenate(hvs, axis=0).astype(_BF16), wcat_ref[0])

    def gates(tb, hvs, g_half):
        t0 = tb * GATE_STEPS
        for j, hv in enumerate(hvs):
            rows = pl.ds((t0 + j) * n_chunks, n_chunks)
            th = jnp.tanh(g_half[j * n_chunks:(j + 1) * n_chunks, :] + bcat_ref[0])
            for d, (a_ref, u_ref, _) in enumerate(directions):
                th_r = th[:, (2 * d) * hd:(2 * d + 1) * hd]
                th_i = th[:, (2 * d + 1) * hd:(2 * d + 2) * hd]
                hr = half_rate[d:d + 1, :]
                log2_a = hr + hr * th_r
                a = jnp.exp2(log2_a)
                x = jnp.tanh(log2_a * (-LN_2)) * (1.0 + a * a)
                scale = jnp.where(x > 0.0, x * lax.rsqrt(x), 0.0)
                a_ref[rows, :] = a
                u_ref[rows, :] = scale * (1.0 + th_i) * hv

    def gate_piece(n):
        w_ref = (gw0_ref, gw1_ref, gw2_ref, gw3_ref)[n // pieces_per_ref]
        c0 = (n % pieces_per_ref) * GATE_PIECE
        cols = pl.ds(n * GATE_PIECE, GATE_PIECE)
        z = _dot(tok_ref[...], w_ref[:, c0:c0 + GATE_PIECE])
        for r in range(0, z.shape[0], PIECE_ROWS):
            gates_ref[pl.ds(r, PIECE_ROWS), cols] = _sigmoid(
                z[r:r + PIECE_ROWS, :] + gbias_ref[:, cols]).astype(gates_ref.dtype)

    n_pieces = gates_ref.shape[1] // GATE_PIECE
    pieces_per_ref = gw0_ref.shape[1] // GATE_PIECE
    n_blocks = SCAN_CHUNK // GATE_STEPS
    assert n_blocks % n_pieces == 0
    logits = [gate_logits(tb) for tb in range(GATE_LOOKAHEAD)]
    for tb in range(n_blocks):
        if tb + GATE_LOOKAHEAD < n_blocks:
            logits.append(gate_logits(tb + GATE_LOOKAHEAD))
        gates(tb, *logits.pop(0))
        if (tb + 1) % (n_blocks // n_pieces) == 0:
            gate_piece(tb // (n_blocks // n_pieces))

    def step_rows(d, g, i):
        t = i if d == 0 else SCAN_CHUNK - 1 - i
        return pl.ds(pl.multiple_of(t * n_chunks, n_chunks) + g * SUBLANES, SUBLANES)

    def chunk_rows(d, g, i):
        t = i if d == 0 else SCAN_CHUNK - 1 - i
        return pl.ds(g * SUBLANES * SCAN_PITCH + t, SUBLANES, stride=SCAN_PITCH)

    lanes = [(d, g) for d in range(2) for g in range(n_groups)]

    def totals(i, carry):
        out = []
        for (d, g), (prod, resp) in zip(lanes, carry):
            a = directions[d][0][step_rows(d, g, i), :]
            u = directions[d][1][step_rows(d, g, i), :]
            out.append((prod * a, a * resp + u))
        return tuple(out)

    init = tuple((jnp.ones((SUBLANES, hd), _F32), jnp.zeros((SUBLANES, hd), _F32)) for _ in lanes)
    tot = lax.fori_loop(0, SCAN_CHUNK, totals, init, unroll=SCAN_UNROLL)
    for (d, g), (prod, resp) in zip(lanes, tot):
        tot_ref[2 * d, pl.ds(g * SUBLANES, SUBLANES), :] = prod
        tot_ref[2 * d + 1, pl.ds(g * SUBLANES, SUBLANES), :] = resp

    for d in range(2):
        state = jnp.zeros((1, hd), _F32)
        for c in (range(n_chunks) if d == 0 else range(n_chunks - 1, -1, -1)):
            cin_ref[d, pl.ds(c, 1), :] = state
            state = tot_ref[2 * d, pl.ds(c, 1), :] * state + tot_ref[2 * d + 1, pl.ds(c, 1), :]

    def replay(i, carry):
        out = []
        for (d, g), h_in in zip(lanes, carry):
            a = directions[d][0][step_rows(d, g, i), :]
            u = directions[d][1][step_rows(d, g, i), :]
            h = a * h_in + u
            directions[d][2][chunk_rows(d, g, i), :] = h
            out.append(h)
        return tuple(out)

    lax.fori_loop(0, SCAN_CHUNK, replay,
                  tuple(cin_ref[d, pl.ds(g * SUBLANES, SUBLANES), :] for d, g in lanes), unroll=SCAN_UNROLL)

    def finish(c, carry):
        r0 = pl.multiple_of(c * SCAN_CHUNK, SCAN_CHUNK)
        p0 = pl.multiple_of(c * SCAN_PITCH, SUBLANES)
        h = hf_ref[pl.ds(p0, SCAN_CHUNK), :] + hb_ref[pl.ds(p0, SCAN_CHUNK), :]
        zg = zg_ref[0, pl.ds(r0, SCAN_CHUNK), :].astype(_F32)
        o_ref[0, pl.ds(r0, SCAN_CHUNK), :] = (h * _gelu_tanh(zg)).astype(o_ref.dtype)
        return carry

    lax.fori_loop(0, n_chunks, finish, 0, unroll=2)


def _lru_call(z_rec, z_gelu, conv_w, conv_b, wcat, bcat, lam, tokens, w_in, b_gates):
    b, s, w = z_rec.shape
    heads = w // HEAD_DIM
    n_chunks = s // SCAN_CHUNK
    assert s % (SCAN_CHUNK * SUBLANES) == 0
    t, d = tokens.shape
    tile = t // (b * heads)
    n_wrefs = 4
    wcols = GATE_WIDTH // n_wrefs
    first = (IN_WIDTH - GATE_WIDTH) // wcols
    assert (IN_WIDTH - GATE_WIDTH) % wcols == 0 and t % (b * heads) == 0
    blk = pl.BlockSpec((1, s, HEAD_DIM), lambda i, h: (i, 0, h))
    pitched = pltpu.VMEM((n_chunks * SCAN_PITCH, HEAD_DIM), _F32)
    time_major = pltpu.VMEM((s, HEAD_DIM), _F32)
    gate_w_specs = [
        pl.BlockSpec((d, wcols), lambda i, h, c=first + k: (0, c), pipeline_mode=pl.Buffered(1))
        for k in range(n_wrefs)]
    return pl.pallas_call(
        _lru_kernel,
        grid=(b, heads),
        in_specs=[
            blk,
            blk,
            pl.BlockSpec((conv_w.shape[0], HEAD_DIM), lambda i, h: (0, h)),
            pl.BlockSpec((1, HEAD_DIM), lambda i, h: (0, h)),
            pl.BlockSpec((1, HEAD_DIM, 4 * HEAD_DIM), lambda i, h: (h, 0, 0)),
            pl.BlockSpec((1, 1, 4 * HEAD_DIM), lambda i, h: (h, 0, 0)),
            pl.BlockSpec((1, 2, HEAD_DIM), lambda i, h: (h, 0, 0)),
            pl.BlockSpec((tile, d), lambda i, h: (i * heads + h, 0)),
            *gate_w_specs,
            pl.BlockSpec((1, GATE_WIDTH), lambda i, h: (0, 0)),
        ],
        out_specs=(blk, pl.BlockSpec((tile, GATE_WIDTH), lambda i, h: (i * heads + h, 0))),
        out_shape=(jax.ShapeDtypeStruct((b, s, w), _BF16), jax.ShapeDtypeStruct((t, GATE_WIDTH), _BF16)),
        scratch_shapes=[
            pltpu.VMEM((n_chunks * CONV_PITCH, HEAD_DIM), _F32),
            time_major, time_major, time_major, time_major, pitched, pitched,
            pltpu.VMEM((4, n_chunks, HEAD_DIM), _F32),
            pltpu.VMEM((2, n_chunks, HEAD_DIM), _F32),
        ],
        compiler_params=pltpu.CompilerParams(
            dimension_semantics=("parallel", "parallel"), vmem_limit_bytes=VMEM_LIMIT_BYTES),
        name="rglru_gates",
    )(z_rec, z_gelu, conv_w, conv_b, wcat, bcat, lam, tokens, w_in, w_in, w_in, w_in, b_gates)


def _merge_kernel(x_ref, yf_ref, yg_ref, gates_ref, pa_ref, pb_ref, wo_ref, *refs):
    n_casts = (len(refs) - 1) // 2
    cast_in, o_ref, cast_out = refs[:n_casts], refs[n_casts], refs[n_casts + 1:]
    d = x_ref.shape[1]
    y_a = _dot(yf_ref[...], pa_ref[...])
    y_b = _dot(yg_ref[...], pb_ref[...])
    merged = gates_ref[:, :d].astype(_F32) * y_a + gates_ref[:, d:].astype(_F32) * y_b
    o_ref[...] = x_ref[...] + _dot(merged.astype(_BF16), wo_ref[...])
    _cast_blocks(cast_in, cast_out)


def _merge_call(x, y_four, y_rec, gates, proj_a, proj_b, w_out, *, casts=(), tm=256):
    t, d = x.shape
    grid = (t // tm,)
    resident = functools.partial(pl.BlockSpec, index_map=lambda i: (0, 0), pipeline_mode=pl.Buffered(1))
    row_spec = pl.BlockSpec((tm, d), lambda i: (i, 0))
    cast_specs, cast_shapes = _ride_along_casts(casts, grid)
    return pl.pallas_call(
        _merge_kernel,
        grid=grid,
        in_specs=[
            row_spec,
            pl.BlockSpec((tm, y_four.shape[1]), lambda i: (i, 0)),
            pl.BlockSpec((tm, y_rec.shape[1]), lambda i: (i, 0)),
            pl.BlockSpec((tm, gates.shape[1]), lambda i: (i, 0)),
            resident(proj_a.shape),
            resident(proj_b.shape),
            resident(w_out.shape),
            *cast_specs,
        ],
        out_specs=(row_spec, *cast_specs),
        out_shape=(jax.ShapeDtypeStruct((t, d), _F32), *cast_shapes),
        compiler_params=pltpu.CompilerParams(
            dimension_semantics=("arbitrary",), vmem_limit_bytes=VMEM_LIMIT_BYTES),
        name="merge",
    )(x, y_four, y_rec, gates, proj_a, proj_b, w_out, *casts)


def kernel(x, ffn1_norm, ffn1_w_gate, ffn1_w_up, ffn1_w_down, mix_norm, w_in, b_gates, conv_w, conv_b, lru_wa, lru_ba, lru_wx, lru_bx, lru_lambda, proj_a, proj_b, w_out, ffn2_norm, ffn2_w_gate, ffn2_w_up, ffn2_w_down, final_norm):
    b, s, d = x.shape
    assert ffn1_norm.shape[0] == 1, "single-layer problem"
    bf = lambda w: w.astype(_BF16)
    x1, u, w_in_b = _ffn_call(
        x.reshape(b * s, d), ffn1_norm, bf(ffn1_w_gate[0]), bf(ffn1_w_up[0]), bf(ffn1_w_down[0]),
        mix_norm, emit_residual=True, casts=(w_in[0],), tm=512)
    z_four, z_rec, z_gelu, wg2, proj_a_b, proj_b_b, w_out_b = _inproj_call(
        u, w_in_b, casts=(ffn2_w_gate[0], proj_a[0], proj_b[0], w_out[0]))

    y_four, wd2 = _fourier_call(z_four.reshape(b, s, FOURIER_WIDTH), casts=(ffn2_w_down[0],))

    wcat = jnp.concatenate([lru_wa[0, 0], lru_wx[0, 0], lru_wa[0, 1], lru_wx[0, 1]], axis=-1)
    bcat = 0.5 * jnp.concatenate([lru_ba[0, 0], lru_bx[0, 0], lru_ba[0, 1], lru_bx[0, 1]], axis=-1)
    lam = jnp.transpose(lru_lambda[0], (1, 0, 2))
    y_rec, gates = _lru_call(z_rec.reshape(b, s, LRU_WIDTH), z_gelu.reshape(b, s, LRU_WIDTH),
                             conv_w[0], conv_b, bf(wcat), bcat[:, None, :], lam,
                             u, w_in_b, b_gates.reshape(1, GATE_WIDTH))

    x2, wu2 = _merge_call(x1, y_four.reshape(b * s, FOURIER_WIDTH), y_rec.reshape(b * s, LRU_WIDTH), gates,
                          proj_a_b, proj_b_b, w_out_b, casts=(ffn2_w_up[0],))
    (out,) = _ffn_call(x2, ffn2_norm, wg2, wu2, wd2, final_norm[None], emit_residual=False, tm=512)
    return out.reshape(b, s, d)
```

```python
import functools
import math

import numpy as np
import jax
import jax.numpy as jnp
from jax import lax
from jax.experimental import pallas as pl
from jax.experimental.pallas import tpu as pltpu

D_MODEL = 2048
D_FF = 5632
FOURIER_WIDTH = 1024
FOURIER_GROUPS = 4
GROUP_DIM = FOURIER_WIDTH // FOURIER_GROUPS
LRU_WIDTH = 1024
LRU_HEADS = 8
HEAD_DIM = LRU_WIDTH // LRU_HEADS
LRU_C = 8.0
RMS_EPS = 1e-6
FFN_RES_SCALE = 0.5
GATE_WIDTH = 2 * D_MODEL
IN_WIDTH = FOURIER_WIDTH + 2 * LRU_WIDTH + GATE_WIDTH

VMEM_LIMIT_BYTES = 56 * 1024 * 1024
SUBLANES = 8
LANES = 128
RING_SLOTS = 3
CAST_ROWS = 2 * SUBLANES

FFT_BLOCKS = 16
SCAN_CHUNK = 128
SCAN_PITCH = SCAN_CHUNK + SUBLANES
CONV_PITCH = SCAN_CHUNK + 3 * SUBLANES
PIECE_ROWS = 64
GATE_STEPS = 8
GATE_PIECE = 256
SCAN_UNROLL = 4
GATE_LOOKAHEAD = 1
LOG2_E = 1.0 / math.log(2.0)
LN_2 = math.log(2.0)

_BF16 = jnp.bfloat16
_F32 = jnp.float32


def _dot(a, b):
    return jnp.dot(a, b, preferred_element_type=_F32)


def _sigmoid(x):
    return 0.5 * (1.0 + jnp.tanh(0.5 * x))


def _rms_norm(x, gain):
    ms = jnp.mean(x * x, axis=-1, keepdims=True)
    return x * lax.rsqrt(ms + RMS_EPS) * gain


def _gelu_tanh(x):
    return 0.5 * x * (1.0 + jnp.tanh(math.sqrt(2.0 / math.pi) * (x + 0.044715 * (x * x * x))))


def _ride_along_casts(weights, grid):
    n_steps = math.prod(grid)
    specs, shapes = [], []
    for w in weights:
        rows, cols = w.shape
        block_rows = next(r for r in range(CAST_ROWS, rows + 1, CAST_ROWS)
                          if rows % r == 0 and rows // r <= n_steps)
        n_blocks = rows // block_rows

        def index_map(*idx, n_blocks=n_blocks):
            step = 0
            for i, extent in zip(idx, grid):
                step = step * extent + i
            return (step * n_blocks // n_steps, 0)

        specs.append(pl.BlockSpec((block_rows, cols), index_map))
        shapes.append(jax.ShapeDtypeStruct((rows, cols), _BF16))
    return specs, shapes


def _cast_blocks(srcs, dsts):
    for src, dst in zip(srcs, dsts):
        dst[...] = src[...].astype(_BF16)


def _ffn_kernel(*refs, emit_residual, n_casts):
    x_ref, gain_ref, wg_ref, wu_ref, wd_ref, post_ref = refs[:6]
    cast_in = refs[6:6 + n_casts]
    outs = refs[6 + n_casts:]
    if emit_residual:
        res_ref, normed_ref = outs[:2]
        outs = outs[2:]
    else:
        normed_ref = outs[0]
        outs = outs[1:]
    cast_out = outs[:n_casts]
    (h_ref,) = outs[n_casts:]
    acc_ref = res_ref if emit_residual else normed_ref
    j = pl.program_id(1)

    @pl.when(j == 0)
    def _():
        h_ref[...] = _rms_norm(x_ref[...], gain_ref[...]).astype(_BF16)
        acc_ref[...] = jnp.zeros_like(acc_ref)

    h = h_ref[...]
    g = _dot(h, wg_ref[...])
    u = _dot(h, wu_ref[...])
    act = (g * _sigmoid(g) * u).astype(_BF16)
    acc_ref[...] += _dot(act, wd_ref[...])

    _cast_blocks(cast_in, cast_out)

    @pl.when(j == pl.num_programs(1) - 1)
    def _():
        y = x_ref[...] + FFN_RES_SCALE * acc_ref[...]
        if emit_residual:
            res_ref[...] = y
        normed_ref[...] = _rms_norm(y, post_ref[...]).astype(normed_ref.dtype)


def _ffn_ring_kernel(x_ref, gain_ref, wg_hbm, wu_hbm, wd_hbm, post_ref, out_ref,
                     h_ref, wg_buf, wu_buf, wd_buf, sems, *, tf):
    n_f = pl.num_programs(1)
    j = pl.program_id(1)
    step = pl.program_id(0) * n_f + j
    last_step = pl.num_programs(0) * n_f - 1

    def tile_copies(s):
        slot = s % RING_SLOTS
        col = pl.multiple_of((s % n_f) * tf, tf)
        return (pltpu.make_async_copy(wg_hbm.at[:, pl.ds(col, tf)], wg_buf.at[slot], sems.at[0, slot]),
                pltpu.make_async_copy(wu_hbm.at[:, pl.ds(col, tf)], wu_buf.at[slot], sems.at[1, slot]),
                pltpu.make_async_copy(wd_hbm.at[pl.ds(col, tf), :], wd_buf.at[slot], sems.at[2, slot]))

    @pl.when(step == 0)
    def _():
        for s in range(RING_SLOTS - 1):
            for copy in tile_copies(s):
                copy.start()

    @pl.when(step + RING_SLOTS - 1 <= last_step)
    def _():
        for copy in tile_copies(step + RING_SLOTS - 1):
            copy.start()

    @pl.when(j == 0)
    def _():
        h_ref[...] = _rms_norm(x_ref[...], gain_ref[...]).astype(_BF16)
        out_ref[...] = jnp.zeros_like(out_ref)

    for copy in tile_copies(step):
        copy.wait()
    slot = step % RING_SLOTS
    h = h_ref[...]
    g = _dot(h, wg_buf[slot])
    u = _dot(h, wu_buf[slot])
    act = (g * _sigmoid(g) * u).astype(_BF16)
    out_ref[...] += _dot(act, wd_buf[slot])

    @pl.when(j == n_f - 1)
    def _():
        y = x_ref[...] + FFN_RES_SCALE * out_ref[...]
        out_ref[...] = _rms_norm(y, post_ref[...])


def _ffn_ring_call(x, gain, wg, wu, wd, post_gain, *, tm, tf=512):
    t, d = x.shape
    f = wg.shape[1]
    assert (t // tm) * (f // tf) >= RING_SLOTS
    row_spec = pl.BlockSpec((tm, d), lambda i, j: (i, 0))
    vec_spec = pl.BlockSpec((1, d), lambda i, j: (0, 0))
    hbm = pl.BlockSpec(memory_space=pl.ANY)
    return pl.pallas_call(
        functools.partial(_ffn_ring_kernel, tf=tf),
        grid=(t // tm, f // tf),
        in_specs=[row_spec, vec_spec, hbm, hbm, hbm, vec_spec],
        out_specs=row_spec,
        out_shape=jax.ShapeDtypeStruct((t, d), _F32),
        scratch_shapes=[
            pltpu.VMEM((tm, d), _BF16),
            pltpu.VMEM((RING_SLOTS, d, tf), _BF16),
            pltpu.VMEM((RING_SLOTS, d, tf), _BF16),
            pltpu.VMEM((RING_SLOTS, tf, d), _BF16),
            pltpu.SemaphoreType.DMA((3, RING_SLOTS)),
        ],
        compiler_params=pltpu.CompilerParams(
            dimension_semantics=("arbitrary", "arbitrary"), vmem_limit_bytes=VMEM_LIMIT_BYTES),
        name="ffn_final",
    )(x, gain, wg, wu, wd, post_gain)


def _ffn_call(x, gain, wg, wu, wd, post_gain, *, emit_residual, casts=(), tm, tf=512):
    t, d = x.shape
    f = wg.shape[1]
    grid = (t // tm, f // tf)
    row_spec = pl.BlockSpec((tm, d), lambda i, j: (i, 0))
    vec_spec = pl.BlockSpec((1, d), lambda i, j: (0, 0))
    in_specs = [
        row_spec,
        vec_spec,
        pl.BlockSpec((d, tf), lambda i, j: (0, j)),
        pl.BlockSpec((d, tf), lambda i, j: (0, j)),
        pl.BlockSpec((tf, d), lambda i, j: (j, 0)),
        vec_spec,
    ]
    out_shape = [jax.ShapeDtypeStruct((t, d), _BF16 if emit_residual else _F32)]
    out_specs = [row_spec]
    if emit_residual:
        out_shape.insert(0, jax.ShapeDtypeStruct((t, d), _F32))
        out_specs.insert(0, row_spec)
    cast_specs, cast_shapes = _ride_along_casts(casts, grid)
    in_specs += cast_specs
    out_specs += cast_specs
    out_shape += cast_shapes
    return pl.pallas_call(
        functools.partial(_ffn_kernel, emit_residual=emit_residual, n_casts=len(casts)),
        grid=grid,
        in_specs=in_specs,
        out_specs=tuple(out_specs),
        out_shape=tuple(out_shape),
        scratch_shapes=[pltpu.VMEM((tm, d), _BF16)],
        compiler_params=pltpu.CompilerParams(
            dimension_semantics=("arbitrary", "arbitrary"), vmem_limit_bytes=VMEM_LIMIT_BYTES),
        name="ffn_residual" if emit_residual else "ffn_final",
    )(x, gain, wg, wu, wd, post_gain, *casts)


def _inproj_kernel(*refs, n_casts):
    u_ref, w_ref = refs[:2]
    cast_in = refs[2:2 + n_casts]
    mixer_refs = refs[2 + n_casts:5 + n_casts]
    cast_out = refs[5 + n_casts:]
    j = pl.program_id(1)
    for col, ref in enumerate(mixer_refs):
        @pl.when(j == col)
        def _(ref=ref):
            ref[...] = _dot(u_ref[...], w_ref[...]).astype(_BF16)
            _cast_blocks(cast_in, cast_out)


def _inproj_call(u, w_in, *, casts=(), tm=1024, tn=1024):
    t, d = u.shape
    assert FOURIER_WIDTH == LRU_WIDTH == tn
    grid = (t // tm, 3)
    mixer_spec = pl.BlockSpec((tm, tn), lambda i, j: (i, 0))
    cast_specs, cast_shapes = _ride_along_casts(casts, grid)
    return pl.pallas_call(
        functools.partial(_inproj_kernel, n_casts=len(casts)),
        grid=grid,
        in_specs=[
            pl.BlockSpec((tm, d), lambda i, j: (i, 0)),
            pl.BlockSpec((d, tn), lambda i, j: (0, j)),
        ] + cast_specs,
        out_specs=(mixer_spec, mixer_spec, mixer_spec, *cast_specs),
        out_shape=(
            jax.ShapeDtypeStruct((t, FOURIER_WIDTH), _BF16),
            jax.ShapeDtypeStruct((t, LRU_WIDTH), _BF16),
            jax.ShapeDtypeStruct((t, LRU_WIDTH), _BF16),
            *cast_shapes,
        ),
        compiler_params=pltpu.CompilerParams(
            dimension_semantics=("arbitrary", "arbitrary"), vmem_limit_bytes=VMEM_LIMIT_BYTES),
        name="inproj",
    )(u, w_in, *casts)


def _fourier_constants(seq):
    rows = seq // FFT_BLOCKS
    c = np.arange(GROUP_DIM)
    ang_c = 2.0 * np.pi * np.outer(c, c) / GROUP_DIM
    chan = np.concatenate([np.cos(ang_c), -np.sin(ang_c)], axis=1) / math.sqrt(GROUP_DIM)
    n1 = np.arange(rows)
    k1 = np.arange(rows)
    mats = []
    for k2 in range(FFT_BLOCKS):
        ang = 2.0 * np.pi * np.outer(FFT_BLOCKS * k1 + k2, n1) / seq
        mats.append(np.concatenate([np.cos(ang), np.sin(ang)], axis=1) / math.sqrt(seq))
    seq_mats = np.stack(mats)
    tile = FFT_BLOCKS * FFT_BLOCKS
    rho = np.arange(tile)
    perm = np.zeros((tile, tile), np.float32)
    perm[rho, FFT_BLOCKS * (rho % FFT_BLOCKS) + rho // FFT_BLOCKS] = 1.0
    return (jnp.asarray(chan, _F32), jnp.asarray(seq_mats, _F32), jnp.asarray(perm, _F32))


def _fft_across_blocks(xs):
    n = len(xs)
    if n == 1:
        return xs
    ev = _fft_across_blocks(xs[0::2])
    od = _fft_across_blocks(xs[1::2])
    out = [None] * n
    for k in range(n // 2):
        o_r, o_i = od[k]
        e_r, e_i = ev[k]
        if k == 0:
            t_r, t_i = o_r, o_i
        elif 4 * k == n:
            out[k] = (e_r + o_i, e_i - o_r)
            out[k + n // 2] = (e_r - o_i, e_i + o_r)
            continue
        else:
            c = math.cos(2.0 * math.pi * k / n)
            s = -math.sin(2.0 * math.pi * k / n)
            t_r = c * o_r - s * o_i
            t_i = c * o_i + s * o_r
        out[k] = (e_r + t_r, e_i + t_i)
        out[k + n // 2] = (e_r - t_r, e_i - t_i)
    return out


def _fourier_kernel(z_ref, chan_ref, seq_ref, perm_ref, *refs):
    n_casts = (len(refs) - 4) // 2
    cast_in, o_ref, cast_out = refs[:n_casts], refs[n_casts], refs[n_casts + 1:2 * n_casts + 1]
    w_ref, b_ref, g_ref = refs[2 * n_casts + 1:]
    _cast_blocks(cast_in, cast_out)
    seq = z_ref.shape[1]
    rows = seq // FFT_BLOCKS
    gd = GROUP_DIM

    for blk in range(FFT_BLOCKS):
        sl = pl.ds(blk * rows, rows)
        w_ref[sl, :] = _dot(z_ref[0, sl, :], chan_ref[...])

    def butterfly(r, carry):
        r0 = pl.multiple_of(r * SUBLANES, SUBLANES)
        for lc in range(gd // LANES):
            re_l = pl.ds(lc * LANES, LANES)
            im_l = pl.ds(gd + lc * LANES, LANES)
            xs = [(w_ref[pl.ds(blk * rows + r0, SUBLANES), re_l],
                   w_ref[pl.ds(blk * rows + r0, SUBLANES), im_l]) for blk in range(FFT_BLOCKS)]
            ys = _fft_across_blocks(xs)
            for k2 in range(FFT_BLOCKS):
                b_ref[k2, pl.ds(r0, SUBLANES), pl.ds(lc * LANES, LANES)] = ys[k2][0]
                b_ref[k2, pl.ds(rows + r0, SUBLANES), pl.ds(lc * LANES, LANES)] = ys[k2][1]
        return carry

    lax.fori_loop(0, rows // SUBLANES, butterfly, 0, unroll=2)

    for k2 in range(FFT_BLOCKS):
        yk = _dot(seq_ref[k2], b_ref[k2].astype(_BF16)).astype(_BF16)
        for a in range(rows // FFT_BLOCKS):
            g_ref[a, pl.ds(FFT_BLOCKS * k2, FFT_BLOCKS), :] = yk[FFT_BLOCKS * a:FFT_BLOCKS * (a + 1), :]

    tile = FFT_BLOCKS * FFT_BLOCKS
    for a in range(rows // FFT_BLOCKS):
        o_ref[0, pl.ds(a * tile, tile), :] = _dot(perm_ref[...], g_ref[a]).astype(o_ref.dtype)


def _fourier_call(z_four, *, casts=()):
    b, s, w = z_four.shape
    rows = s // FFT_BLOCKS
    assert rows == FFT_BLOCKS * FFT_BLOCKS and w % GROUP_DIM == 0
    chan, seq_mats, perm = (c.astype(_BF16) for c in _fourier_constants(s))
    blk = pl.BlockSpec((1, s, GROUP_DIM), lambda i, g: (i, 0, g))
    grid = (b, w // GROUP_DIM)
    cast_specs, cast_shapes = _ride_along_casts(casts, grid)
    return pl.pallas_call(
        _fourier_kernel,
        grid=grid,
        in_specs=[
            blk,
            pl.BlockSpec(chan.shape, lambda i, g: (0, 0)),
            pl.BlockSpec(seq_mats.shape, lambda i, g: (0, 0, 0)),
            pl.BlockSpec(perm.shape, lambda i, g: (0, 0)),
            *cast_specs,
        ],
        out_specs=(blk, *cast_specs),
        out_shape=(jax.ShapeDtypeStruct((b, s, w), _BF16), *cast_shapes),
        scratch_shapes=[
            pltpu.VMEM((s, 2 * GROUP_DIM), _F32),
            pltpu.VMEM((FFT_BLOCKS, 2 * rows, GROUP_DIM), _F32),
            pltpu.VMEM((rows // FFT_BLOCKS, FFT_BLOCKS * FFT_BLOCKS, GROUP_DIM), _BF16),
        ],
        compiler_params=pltpu.CompilerParams(
            dimension_semantics=("arbitrary", "arbitrary"), vmem_limit_bytes=VMEM_LIMIT_BYTES),
        name="fourier_mix",
    )(z_four, chan, seq_mats, perm, *casts)


def _softplus(x):
    return jnp.maximum(x, 0.0) + jnp.log1p(jnp.exp(-jnp.abs(x)))


def _lru_kernel(zr_ref, zg_ref, cw_ref, cb_ref, wcat_ref, bcat_ref, lam_ref,
                tok_ref, gw0_ref, gw1_ref, gw2_ref, gw3_ref, gbias_ref, o_ref, gates_ref,
                zpad_ref, af_ref, uf_ref, ab_ref, ub_ref, hf_ref, hb_ref, tot_ref, cin_ref):
    seq = zr_ref.shape[1]
    hd = HEAD_DIM
    n_chunks = seq // SCAN_CHUNK
    n_groups = n_chunks // SUBLANES
    halo = SUBLANES

    def fill(c, carry):
        r0 = pl.multiple_of(c * SCAN_CHUNK, SCAN_CHUNK)
        p0 = pl.multiple_of(c * CONV_PITCH, SUBLANES)
        zpad_ref[pl.ds(p0 + halo, SCAN_CHUNK), :] = zr_ref[0, pl.ds(r0, SCAN_CHUNK), :].astype(_F32)
        return carry

    lax.fori_loop(0, n_chunks, fill, 0)
    for c in range(n_chunks):
        before = (zpad_ref[pl.ds((c - 1) * CONV_PITCH + SCAN_CHUNK, halo), :] if c > 0
                  else jnp.zeros((halo, hd), _F32))
        after = (zpad_ref[pl.ds((c + 1) * CONV_PITCH + halo, halo), :] if c < n_chunks - 1
                 else jnp.zeros((halo, hd), _F32))
        zpad_ref[pl.ds(c * CONV_PITCH, halo), :] = before
        zpad_ref[pl.ds(c * CONV_PITCH + halo + SCAN_CHUNK, halo), :] = after

    cw = 0.5 * cw_ref[...]
    cb = 0.5 * cb_ref[...]
    half_rate = (-0.5 * LRU_C * LOG2_E) * _softplus(-lam_ref[0])
    directions = ((af_ref, uf_ref, hf_ref), (ab_ref, ub_ref, hb_ref))

    def gate_logits(tb):
        t0 = tb * GATE_STEPS
        taps = [zpad_ref[pl.ds(halo - 2 + t0 + m, n_chunks, stride=CONV_PITCH), :]
                for m in range(GATE_STEPS + 3)]
        hvs = [cb + sum(taps[j + k] * cw[k:k + 1, :] for k in range(4)) for j in range(GATE_STEPS)]
        return hvs, _dot(jnp.concatenate(hvs, axis=0).astype(_BF16), wcat_ref[0])

    def gates(tb, hvs, g_half):
        t0 = tb * GATE_STEPS
        for j, hv in enumerate(hvs):
            rows = pl.ds((t0 + j) * n_chunks, n_chunks)
            th = jnp.tanh(g_half[j * n_chunks:(j + 1) * n_chunks, :] + bcat_ref[0])
            for d, (a_ref, u_ref, _) in enumerate(directions):
                th_r = th[:, (2 * d) * hd:(2 * d + 1) * hd]
                th_i = th[:, (2 * d + 1) * hd:(2 * d + 2) * hd]
                hr = half_rate[d:d + 1, :]
                log2_a = hr + hr * th_r
                a = jnp.exp2(log2_a)
                x = jnp.tanh(log2_a * (-LN_2)) * (1.0 + a * a)
                scale = jnp.where(x > 0.0, x * lax.rsqrt(x), 0.0)
                a_ref[rows, :] = a
                u_ref[rows, :] = scale * (1.0 + th_i) * hv

    def gate_piece(n):
        w_ref = (gw0_ref, gw1_ref, gw2_ref, gw3_ref)[n // pieces_per_ref]
        c0 = (n % pieces_per_ref) * GATE_PIECE
        cols = pl.ds(n * GATE_PIECE, GATE_PIECE)
        z = _dot(tok_ref[...], w_ref[:, c0:c0 + GATE_PIECE])
        for r in range(0, z.shape[0], PIECE_ROWS):
            gates_ref[pl.ds(r, PIECE_ROWS), cols] = _sigmoid(
                z[r:r + PIECE_ROWS, :] + gbias_ref[:, cols]).astype(gates_ref.dtype)

    n_pieces = gates_ref.shape[1] // GATE_PIECE
    pieces_per_ref = gw0_ref.shape[1] // GATE_PIECE
    n_blocks = SCAN_CHUNK // GATE_STEPS
    assert n_blocks % n_pieces == 0
    logits = [gate_logits(tb) for tb in range(GATE_LOOKAHEAD)]
    for tb in range(n_blocks):
        if tb + GATE_LOOKAHEAD < n_blocks:
            logits.append(gate_logits(tb + GATE_LOOKAHEAD))
        gates(tb, *logits.pop(0))
        if (tb + 1) % (n_blocks // n_pieces) == 0:
            gate_piece(tb // (n_blocks // n_pieces))

    def step_rows(d, g, i):
        t = i if d == 0 else SCAN_CHUNK - 1 - i
        return pl.ds(pl.multiple_of(t * n_chunks, n_chunks) + g * SUBLANES, SUBLANES)

    def chunk_rows(d, g, i):
        t = i if d == 0 else SCAN_CHUNK - 1 - i
        return pl.ds(g * SUBLANES * SCAN_PITCH + t, SUBLANES, stride=SCAN_PITCH)

    lanes = [(d, g) for d in range(2) for g in range(n_groups)]

    def totals(i, carry):
        out = []
        for (d, g), (prod, resp) in zip(lanes, carry):
            a = directions[d][0][step_rows(d, g, i), :]
            u = directions[d][1][step_rows(d, g, i), :]
            out.append((prod * a, a * resp + u))
        return tuple(out)

    init = tuple((jnp.ones((SUBLANES, hd), _F32), jnp.zeros((SUBLANES, hd), _F32)) for _ in lanes)
    tot = lax.fori_loop(0, SCAN_CHUNK, totals, init, unroll=SCAN_UNROLL)
    for (d, g), (prod, resp) in zip(lanes, tot):
        tot_ref[2 * d, pl.ds(g * SUBLANES, SUBLANES), :] = prod
        tot_ref[2 * d + 1, pl.ds(g * SUBLANES, SUBLANES), :] = resp

    for d in range(2):
        state = jnp.zeros((1, hd), _F32)
        for c in (range(n_chunks) if d == 0 else range(n_chunks - 1, -1, -1)):
            cin_ref[d, pl.ds(c, 1), :] = state
            state = tot_ref[2 * d, pl.ds(c, 1), :] * state + tot_ref[2 * d + 1, pl.ds(c, 1), :]

    def replay(i, carry):
        out = []
        for (d, g), h_in in zip(lanes, carry):
            a = directions[d][0][step_rows(d, g, i), :]
            u = directions[d][1][step_rows(d, g, i), :]
            h = a * h_in + u
            directions[d][2][chunk_rows(d, g, i), :] = h
            out.append(h)
        return tuple(out)

    lax.fori_loop(0, SCAN_CHUNK, replay,
                  tuple(cin_ref[d, pl.ds(g * SUBLANES, SUBLANES), :] for d, g in lanes), unroll=SCAN_UNROLL)

    def finish(c, carry):
        r0 = pl.multiple_of(c * SCAN_CHUNK, SCAN_CHUNK)
        p0 = pl.multiple_of(c * SCAN_PITCH, SUBLANES)
        h = hf_ref[pl.ds(p0, SCAN_CHUNK), :] + hb_ref[pl.ds(p0, SCAN_CHUNK), :]
        zg = zg_ref[0, pl.ds(r0, SCAN_CHUNK), :].astype(_F32)
        o_ref[0, pl.ds(r0, SCAN_CHUNK), :] = (h * _gelu_tanh(zg)).astype(o_ref.dtype)
        return carry

    lax.fori_loop(0, n_chunks, finish, 0)


def _lru_call(z_rec, z_gelu, conv_w, conv_b, wcat, bcat, lam, tokens, w_in, b_gates):
    b, s, w = z_rec.shape
    heads = w // HEAD_DIM
    n_chunks = s // SCAN_CHUNK
    assert s % (SCAN_CHUNK * SUBLANES) == 0
    t, d = tokens.shape
    tile = t // (b * heads)
    n_wrefs = 4
    wcols = GATE_WIDTH // n_wrefs
    first = (IN_WIDTH - GATE_WIDTH) // wcols
    assert (IN_WIDTH - GATE_WIDTH) % wcols == 0 and t % (b * heads) == 0
    blk = pl.BlockSpec((1, s, HEAD_DIM), lambda i, h: (i, 0, h))
    pitched = pltpu.VMEM((n_chunks * SCAN_PITCH, HEAD_DIM), _F32)
    time_major = pltpu.VMEM((s, HEAD_DIM), _F32)
    gate_w_specs = [
        pl.BlockSpec((d, wcols), lambda i, h, c=first + k: (0, c), pipeline_mode=pl.Buffered(1))
        for k in range(n_wrefs)]
    return pl.pallas_call(
        _lru_kernel,
        grid=(b, heads),
        in_specs=[
            blk,
            blk,
            pl.BlockSpec((conv_w.shape[0], HEAD_DIM), lambda i, h: (0, h)),
            pl.BlockSpec((1, HEAD_DIM), lambda i, h: (0, h)),
            pl.BlockSpec((1, HEAD_DIM, 4 * HEAD_DIM), lambda i, h: (h, 0, 0)),
            pl.BlockSpec((1, 1, 4 * HEAD_DIM), lambda i, h: (h, 0, 0)),
            pl.BlockSpec((1, 2, HEAD_DIM), lambda i, h: (h, 0, 0)),
            pl.BlockSpec((tile, d), lambda i, h: (i * heads + h, 0)),
            *gate_w_specs,
            pl.BlockSpec((1, GATE_WIDTH), lambda i, h: (0, 0)),
        ],
        out_specs=(blk, pl.BlockSpec((tile, GATE_WIDTH), lambda i, h: (i * heads + h, 0))),
        out_shape=(jax.ShapeDtypeStruct((b, s, w), _BF16), jax.ShapeDtypeStruct((t, GATE_WIDTH), _BF16)),
        scratch_shapes=[
            pltpu.VMEM((n_chunks * CONV_PITCH, HEAD_DIM), _F32),
            time_major, time_major, time_major, time_major, pitched, pitched,
            pltpu.VMEM((4, n_chunks, HEAD_DIM), _F32),
            pltpu.VMEM((2, n_chunks, HEAD_DIM), _F32),
        ],
        compiler_params=pltpu.CompilerParams(
            dimension_semantics=("parallel", "parallel"), vmem_limit_bytes=VMEM_LIMIT_BYTES),
        name="rglru_gates",
    )(z_rec, z_gelu, conv_w, conv_b, wcat, bcat, lam, tokens, w_in, w_in, w_in, w_in, b_gates)


def _merge_kernel(x_ref, yf_ref, yg_ref, gates_ref, pa_ref, pb_ref, wo_ref, *refs):
    n_casts = (len(refs) - 1) // 2
    cast_in, o_ref, cast_out = refs[:n_casts], refs[n_casts], refs[n_casts + 1:]
    d = x_ref.shape[1]
    y_a = _dot(yf_ref[...], pa_ref[...])
    y_b = _dot(yg_ref[...], pb_ref[...])
    merged = gates_ref[:, :d].astype(_F32) * y_a + gates_ref[:, d:].astype(_F32) * y_b
    o_ref[...] = x_ref[...] + _dot(merged.astype(_BF16), wo_ref[...])
    _cast_blocks(cast_in, cast_out)


def _merge_call(x, y_four, y_rec, gates, proj_a, proj_b, w_out, *, casts=(), tm=256):
    t, d = x.shape
    grid = (t // tm,)
    resident = functools.partial(pl.BlockSpec, index_map=lambda i: (0, 0), pipeline_mode=pl.Buffered(1))
    row_spec = pl.BlockSpec((tm, d), lambda i: (i, 0))
    cast_specs, cast_shapes = _ride_along_casts(casts, grid)
    return pl.pallas_call(
        _merge_kernel,
        grid=grid,
        in_specs=[
            row_spec,
            pl.BlockSpec((tm, y_four.shape[1]), lambda i: (i, 0)),
            pl.BlockSpec((tm, y_rec.shape[1]), lambda i: (i, 0)),
            pl.BlockSpec((tm, gates.shape[1]), lambda i: (i, 0)),
            resident(proj_a.shape),
            resident(proj_b.shape),
            resident(w_out.shape),
            *cast_specs,
        ],
        out_specs=(row_spec, *cast_specs),
        out_shape=(jax.ShapeDtypeStruct((t, d), _F32), *cast_shapes),
        compiler_params=pltpu.CompilerParams(
            dimension_semantics=("arbitrary",), vmem_limit_bytes=VMEM_LIMIT_BYTES),
        name="merge",
    )(x, y_four, y_rec, gates, proj_a, proj_b, w_out, *casts)


def kernel(x, ffn1_norm, ffn1_w_gate, ffn1_w_up, ffn1_w_down, mix_norm, w_in, b_gates, conv_w, conv_b, lru_wa, lru_ba, lru_wx, lru_bx, lru_lambda, proj_a, proj_b, w_out, ffn2_norm, ffn2_w_gate, ffn2_w_up, ffn2_w_down, final_norm):
    b, s, d = x.shape
    assert ffn1_norm.shape[0] == 1, "single-layer problem"
    bf = lambda w: w.astype(_BF16)
    x1, u, w_in_b = _ffn_call(
        x.reshape(b * s, d), ffn1_norm, bf(ffn1_w_gate[0]), bf(ffn1_w_up[0]), bf(ffn1_w_down[0]),
        mix_norm, emit_residual=True, casts=(w_in[0],), tm=512)
    z_four, z_rec, z_gelu, wg2, proj_a_b, proj_b_b, w_out_b = _inproj_call(
        u, w_in_b, casts=(ffn2_w_gate[0], proj_a[0], proj_b[0], w_out[0]))

    y_four, wd2 = _fourier_call(z_four.reshape(b, s, FOURIER_WIDTH), casts=(ffn2_w_down[0],))

    wcat = jnp.concatenate([lru_wa[0, 0], lru_wx[0, 0], lru_wa[0, 1], lru_wx[0, 1]], axis=-1)
    bcat = 0.5 * jnp.concatenate([lru_ba[0, 0], lru_bx[0, 0], lru_ba[0, 1], lru_bx[0, 1]], axis=-1)
    lam = jnp.transpose(lru_lambda[0], (1, 0, 2))
    y_rec, gates = _lru_call(z_rec.reshape(b, s, LRU_WIDTH), z_gelu.reshape(b, s, LRU_WIDTH),
                             conv_w[0], conv_b, bf(wcat), bcat[:, None, :], lam,
                             u, w_in_b, b_gates.reshape(1, GATE_WIDTH))

    x2, wu2 = _merge_call(x1, y_four.reshape(b * s, FOURIER_WIDTH), y_rec.reshape(b * s, LRU_WIDTH), gates,
                          proj_a_b, proj_b_b, w_out_b, casts=(ffn2_w_up[0],))
    out = _ffn_ring_call(x2, ffn2_norm, wg2, wu2, wd2, final_norm[None], tm=512)
    return out.reshape(b, s, d)
```

```python
import functools
import math

import numpy as np
import jax
import jax.numpy as jnp
from jax import lax
from jax.experimental import pallas as pl
from jax.experimental.pallas import tpu as pltpu

D_MODEL = 2048
D_FF = 5632
FOURIER_WIDTH = 1024
FOURIER_GROUPS = 4
GROUP_DIM = FOURIER_WIDTH // FOURIER_GROUPS
LRU_WIDTH = 1024
LRU_HEADS = 8
HEAD_DIM = LRU_WIDTH // LRU_HEADS
LRU_C = 8.0
RMS_EPS = 1e-6
FFN_RES_SCALE = 0.5
GATE_WIDTH = 2 * D_MODEL
IN_WIDTH = FOURIER_WIDTH + 2 * LRU_WIDTH + GATE_WIDTH

VMEM_LIMIT_BYTES = 56 * 1024 * 1024
SUBLANES = 8
LANES = 128
CAST_ROWS = 2 * SUBLANES

FFT_BLOCKS = 16
SCAN_CHUNK = 128
SCAN_PITCH = SCAN_CHUNK + SUBLANES
CONV_PITCH = SCAN_CHUNK + 3 * SUBLANES
PIECE_ROWS = 64
GATE_STEPS = 8
GATE_PIECE = 256
SCAN_UNROLL = 4
GATE_LOOKAHEAD = 1
LOG2_E = 1.0 / math.log(2.0)
LN_2 = math.log(2.0)

_BF16 = jnp.bfloat16
_F32 = jnp.float32


def _dot(a, b):
    return jnp.dot(a, b, preferred_element_type=_F32)


def _sigmoid(x):
    return 0.5 * (1.0 + jnp.tanh(0.5 * x))


def _rms_norm(x, gain):
    ms = jnp.mean(x * x, axis=-1, keepdims=True)
    return x * lax.rsqrt(ms + RMS_EPS) * gain


def _gelu_tanh(x):
    return 0.5 * x * (1.0 + jnp.tanh(math.sqrt(2.0 / math.pi) * (x + 0.044715 * (x * x * x))))


def _ride_along_casts(weights, grid):
    n_steps = math.prod(grid)
    specs, shapes = [], []
    for w in weights:
        rows, cols = w.shape
        block_rows = next(r for r in range(CAST_ROWS, rows + 1, CAST_ROWS)
                          if rows % r == 0 and rows // r <= n_steps)
        n_blocks = rows // block_rows

        def index_map(*idx, n_blocks=n_blocks):
            step = 0
            for i, extent in zip(idx, grid):
                step = step * extent + i
            return (step * n_blocks // n_steps, 0)

        specs.append(pl.BlockSpec((block_rows, cols), index_map))
        shapes.append(jax.ShapeDtypeStruct((rows, cols), _BF16))
    return specs, shapes


def _cast_blocks(srcs, dsts):
    for src, dst in zip(srcs, dsts):
        dst[...] = src[...].astype(_BF16)


def _ffn_kernel(*refs, emit_residual, n_casts):
    x_ref, gain_ref, wg_ref, wu_ref, wd_ref, post_ref = refs[:6]
    cast_in = refs[6:6 + n_casts]
    outs = refs[6 + n_casts:]
    if emit_residual:
        res_ref, normed_ref = outs[:2]
        outs = outs[2:]
    else:
        normed_ref = outs[0]
        outs = outs[1:]
    cast_out = outs[:n_casts]
    (h_ref,) = outs[n_casts:]
    acc_ref = res_ref if emit_residual else normed_ref
    j = pl.program_id(1)

    @pl.when(j == 0)
    def _():
        h_ref[...] = _rms_norm(x_ref[...], gain_ref[...]).astype(_BF16)
        acc_ref[...] = jnp.zeros_like(acc_ref)

    h = h_ref[...]
    g = _dot(h, wg_ref[...])
    u = _dot(h, wu_ref[...])
    act = (g * _sigmoid(g) * u).astype(_BF16)
    acc_ref[...] += _dot(act, wd_ref[...])

    _cast_blocks(cast_in, cast_out)

    @pl.when(j == pl.num_programs(1) - 1)
    def _():
        y = x_ref[...] + FFN_RES_SCALE * acc_ref[...]
        if emit_residual:
            res_ref[...] = y
        normed_ref[...] = _rms_norm(y, post_ref[...]).astype(normed_ref.dtype)


def _ffn_call(x, gain, wg, wu, wd, post_gain, *, emit_residual, casts=(), tm, tf=512):
    t, d = x.shape
    f = wg.shape[1]
    grid = (t // tm, f // tf)
    row_spec = pl.BlockSpec((tm, d), lambda i, j: (i, 0))
    vec_spec = pl.BlockSpec((1, d), lambda i, j: (0, 0))
    in_specs = [
        row_spec,
        vec_spec,
        pl.BlockSpec((d, tf), lambda i, j: (0, j)),
        pl.BlockSpec((d, tf), lambda i, j: (0, j)),
        pl.BlockSpec((tf, d), lambda i, j: (j, 0)),
        vec_spec,
    ]
    out_shape = [jax.ShapeDtypeStruct((t, d), _BF16 if emit_residual else _F32)]
    out_specs = [row_spec]
    if emit_residual:
        out_shape.insert(0, jax.ShapeDtypeStruct((t, d), _F32))
        out_specs.insert(0, row_spec)
    cast_specs, cast_shapes = _ride_along_casts(casts, grid)
    in_specs += cast_specs
    out_specs += cast_specs
    out_shape += cast_shapes
    return pl.pallas_call(
        functools.partial(_ffn_kernel, emit_residual=emit_residual, n_casts=len(casts)),
        grid=grid,
        in_specs=in_specs,
        out_specs=tuple(out_specs),
        out_shape=tuple(out_shape),
        scratch_shapes=[pltpu.VMEM((tm, d), _BF16)],
        compiler_params=pltpu.CompilerParams(
            dimension_semantics=("arbitrary", "arbitrary"), vmem_limit_bytes=VMEM_LIMIT_BYTES),
        name="ffn_residual" if emit_residual else "ffn_final",
    )(x, gain, wg, wu, wd, post_gain, *casts)


def _inproj_kernel(*refs, n_casts):
    u_ref, w_ref = refs[:2]
    cast_in = refs[2:2 + n_casts]
    mixer_refs = refs[2 + n_casts:5 + n_casts]
    cast_out = refs[5 + n_casts:]
    j = pl.program_id(0)
    for col, ref in enumerate(mixer_refs):
        @pl.when(j == col)
        def _(ref=ref):
            ref[...] = _dot(u_ref[...], w_ref[...]).astype(_BF16)
            _cast_blocks(cast_in, cast_out)


def _inproj_call(u, w_in, *, casts=(), tm=1024, tn=1024):
    t, d = u.shape
    assert FOURIER_WIDTH == LRU_WIDTH == tn
    n_rows = t // tm
    grid = (3, n_rows)

    def mixer_spec(col):
        return pl.BlockSpec(
            (tm, tn), lambda j, i: (jnp.where(j == col, i, jnp.where(j < col, 0, n_rows - 1)), 0))

    cast_specs, cast_shapes = _ride_along_casts(casts, grid)
    return pl.pallas_call(
        functools.partial(_inproj_kernel, n_casts=len(casts)),
        grid=grid,
        in_specs=[
            pl.BlockSpec((tm, d), lambda j, i: (i, 0)),
            pl.BlockSpec((d, tn), lambda j, i: (0, j)),
        ] + cast_specs,
        out_specs=(mixer_spec(0), mixer_spec(1), mixer_spec(2), *cast_specs),
        out_shape=(
            jax.ShapeDtypeStruct((t, FOURIER_WIDTH), _BF16),
            jax.ShapeDtypeStruct((t, LRU_WIDTH), _BF16),
            jax.ShapeDtypeStruct((t, LRU_WIDTH), _BF16),
            *cast_shapes,
        ),
        compiler_params=pltpu.CompilerParams(
            dimension_semantics=("arbitrary", "arbitrary"), vmem_limit_bytes=VMEM_LIMIT_BYTES),
        name="inproj",
    )(u, w_in, *casts)


def _fourier_constants(seq):
    rows = seq // FFT_BLOCKS
    c = np.arange(GROUP_DIM)
    ang_c = 2.0 * np.pi * np.outer(c, c) / GROUP_DIM
    chan = np.concatenate([np.cos(ang_c), -np.sin(ang_c)], axis=1) / math.sqrt(GROUP_DIM)
    n1 = np.arange(rows)
    k1 = np.arange(rows)
    mats = []
    for k2 in range(FFT_BLOCKS):
        ang = 2.0 * np.pi * np.outer(FFT_BLOCKS * k1 + k2, n1) / seq
        mats.append(np.concatenate([np.cos(ang), np.sin(ang)], axis=1) / math.sqrt(seq))
    seq_mats = np.stack(mats)
    tile = FFT_BLOCKS * FFT_BLOCKS
    rho = np.arange(tile)
    perm = np.zeros((tile, tile), np.float32)
    perm[rho, FFT_BLOCKS * (rho % FFT_BLOCKS) + rho // FFT_BLOCKS] = 1.0
    return (jnp.asarray(chan, _F32), jnp.asarray(seq_mats, _F32), jnp.asarray(perm, _F32))


def _fft_across_blocks(xs):
    n = len(xs)
    if n == 1:
        return xs
    ev = _fft_across_blocks(xs[0::2])
    od = _fft_across_blocks(xs[1::2])
    out = [None] * n
    for k in range(n // 2):
        o_r, o_i = od[k]
        e_r, e_i = ev[k]
        if k == 0:
            t_r, t_i = o_r, o_i
        elif 4 * k == n:
            out[k] = (e_r + o_i, e_i - o_r)
            out[k + n // 2] = (e_r - o_i, e_i + o_r)
            continue
        else:
            c = math.cos(2.0 * math.pi * k / n)
            s = -math.sin(2.0 * math.pi * k / n)
            t_r = c * o_r - s * o_i
            t_i = c * o_i + s * o_r
        out[k] = (e_r + t_r, e_i + t_i)
        out[k + n // 2] = (e_r - t_r, e_i - t_i)
    return out


def _fourier_kernel(z_ref, chan_ref, seq_ref, perm_ref, *refs):
    n_casts = (len(refs) - 4) // 2
    cast_in, o_ref, cast_out = refs[:n_casts], refs[n_casts], refs[n_casts + 1:2 * n_casts + 1]
    w_ref, b_ref, g_ref = refs[2 * n_casts + 1:]
    _cast_blocks(cast_in, cast_out)
    seq = z_ref.shape[1]
    rows = seq // FFT_BLOCKS
    gd = GROUP_DIM

    for blk in range(FFT_BLOCKS):
        sl = pl.ds(blk * rows, rows)
        w_ref[sl, :] = _dot(z_ref[0, sl, :], chan_ref[...])

    def butterfly(r, carry):
        r0 = pl.multiple_of(r * SUBLANES, SUBLANES)
        for lc in range(gd // LANES):
            re_l = pl.ds(lc * LANES, LANES)
            im_l = pl.ds(gd + lc * LANES, LANES)
            xs = [(w_ref[pl.ds(blk * rows + r0, SUBLANES), re_l],
                   w_ref[pl.ds(blk * rows + r0, SUBLANES), im_l]) for blk in range(FFT_BLOCKS)]
            ys = _fft_across_blocks(xs)
            for k2 in range(FFT_BLOCKS):
                b_ref[k2, pl.ds(r0, SUBLANES), pl.ds(lc * LANES, LANES)] = ys[k2][0]
                b_ref[k2, pl.ds(rows + r0, SUBLANES), pl.ds(lc * LANES, LANES)] = ys[k2][1]
        return carry

    lax.fori_loop(0, rows // SUBLANES, butterfly, 0, unroll=2)

    for k2 in range(FFT_BLOCKS):
        yk = _dot(seq_ref[k2], b_ref[k2].astype(_BF16)).astype(_BF16)
        for a in range(rows // FFT_BLOCKS):
            g_ref[a, pl.ds(FFT_BLOCKS * k2, FFT_BLOCKS), :] = yk[FFT_BLOCKS * a:FFT_BLOCKS * (a + 1), :]

    tile = FFT_BLOCKS * FFT_BLOCKS
    for a in range(rows // FFT_BLOCKS):
        o_ref[0, pl.ds(a * tile, tile), :] = _dot(perm_ref[...], g_ref[a]).astype(o_ref.dtype)


def _fourier_call(z_four, *, casts=()):
    b, s, w = z_four.shape
    rows = s // FFT_BLOCKS
    assert rows == FFT_BLOCKS * FFT_BLOCKS and w % GROUP_DIM == 0
    chan, seq_mats, perm = (c.astype(_BF16) for c in _fourier_constants(s))
    blk = pl.BlockSpec((1, s, GROUP_DIM), lambda i, g: (i, 0, g))
    grid = (b, w // GROUP_DIM)
    cast_specs, cast_shapes = _ride_along_casts(casts, grid)
    return pl.pallas_call(
        _fourier_kernel,
        grid=grid,
        in_specs=[
            blk,
            pl.BlockSpec(chan.shape, lambda i, g: (0, 0)),
            pl.BlockSpec(seq_mats.shape, lambda i, g: (0, 0, 0)),
            pl.BlockSpec(perm.shape, lambda i, g: (0, 0)),
            *cast_specs,
        ],
        out_specs=(blk, *cast_specs),
        out_shape=(jax.ShapeDtypeStruct((b, s, w), _BF16), *cast_shapes),
        scratch_shapes=[
            pltpu.VMEM((s, 2 * GROUP_DIM), _F32),
            pltpu.VMEM((FFT_BLOCKS, 2 * rows, GROUP_DIM), _F32),
            pltpu.VMEM((rows // FFT_BLOCKS, FFT_BLOCKS * FFT_BLOCKS, GROUP_DIM), _BF16),
        ],
        compiler_params=pltpu.CompilerParams(
            dimension_semantics=("arbitrary", "arbitrary"), vmem_limit_bytes=VMEM_LIMIT_BYTES),
        name="fourier_mix",
    )(z_four, chan, seq_mats, perm, *casts)


def _softplus(x):
    return jnp.maximum(x, 0.0) + jnp.log1p(jnp.exp(-jnp.abs(x)))


def _lru_kernel(zr_ref, zg_ref, cw_ref, cb_ref, wcat_ref, bcat_ref, lam_ref,
                tok_ref, gw0_ref, gw1_ref, gw2_ref, gw3_ref, gbias_ref, o_ref, gates_ref,
                zpad_ref, af_ref, uf_ref, ab_ref, ub_ref, hf_ref, hb_ref, tot_ref, cin_ref):
    seq = zr_ref.shape[1]
    hd = HEAD_DIM
    n_chunks = seq // SCAN_CHUNK
    n_groups = n_chunks // SUBLANES
    halo = SUBLANES

    def fill(c, carry):
        r0 = pl.multiple_of(c * SCAN_CHUNK, SCAN_CHUNK)
        p0 = pl.multiple_of(c * CONV_PITCH, SUBLANES)
        zpad_ref[pl.ds(p0 + halo, SCAN_CHUNK), :] = zr_ref[0, pl.ds(r0, SCAN_CHUNK), :].astype(_F32)
        return carry

    lax.fori_loop(0, n_chunks, fill, 0)
    for c in range(n_chunks):
        before = (zpad_ref[pl.ds((c - 1) * CONV_PITCH + SCAN_CHUNK, halo), :] if c > 0
                  else jnp.zeros((halo, hd), _F32))
        after = (zpad_ref[pl.ds((c + 1) * CONV_PITCH + halo, halo), :] if c < n_chunks - 1
                 else jnp.zeros((halo, hd), _F32))
        zpad_ref[pl.ds(c * CONV_PITCH, halo), :] = before
        zpad_ref[pl.ds(c * CONV_PITCH + halo + SCAN_CHUNK, halo), :] = after

    cw = 0.5 * cw_ref[...]
    cb = 0.5 * cb_ref[...]
    half_rate = (-0.5 * LRU_C * LOG2_E) * _softplus(-lam_ref[0])
    directions = ((af_ref, uf_ref, hf_ref), (ab_ref, ub_ref, hb_ref))

    def gate_logits(tb):
        t0 = tb * GATE_STEPS
        taps = [zpad_ref[pl.ds(halo - 2 + t0 + m, n_chunks, stride=CONV_PITCH), :]
                for m in range(GATE_STEPS + 3)]
        hvs = [cb + sum(taps[j + k] * cw[k:k + 1, :] for k in range(4)) for j in range(GATE_STEPS)]
        return hvs, _dot(jnp.concatenate(hvs, axis=0).astype(_BF16), wcat_ref[0])

    def gates(tb, hvs, g_half):
        t0 = tb * GATE_STEPS
        for j, hv in enumerate(hvs):
            rows = pl.ds((t0 + j) * n_chunks, n_chunks)
            th = jnp.tanh(g_half[j * n_chunks:(j + 1) * n_chunks, :] + bcat_ref[0])
            for d, (a_ref, u_ref, _) in enumerate(directions):
                th_r = th[:, (2 * d) * hd:(2 * d + 1) * hd]
                th_i = th[:, (2 * d + 1) * hd:(2 * d + 2) * hd]
                hr = half_rate[d:d + 1, :]
                log2_a = hr + hr * th_r
                a = jnp.exp2(log2_a)
                x = jnp.tanh(log2_a * (-LN_2)) * (1.0 + a * a)
                scale = jnp.where(x > 0.0, x * lax.rsqrt(x), 0.0)
                a_ref[rows, :] = a
                u_ref[rows, :] = scale * (1.0 + th_i) * hv

    def gate_piece(n):
        w_ref = (gw0_ref, gw1_ref, gw2_ref, gw3_ref)[n // pieces_per_ref]
        c0 = (n % pieces_per_ref) * GATE_PIECE
        cols = pl.ds(n * GATE_PIECE, GATE_PIECE)
        z = _dot(tok_ref[...], w_ref[:, c0:c0 + GATE_PIECE])
        for r in range(0, z.shape[0], PIECE_ROWS):
            gates_ref[pl.ds(r, PIECE_ROWS), cols] = _sigmoid(
                z[r:r + PIECE_ROWS, :] + gbias_ref[:, cols]).astype(gates_ref.dtype)

    n_pieces = gates_ref.shape[1] // GATE_PIECE
    pieces_per_ref = gw0_ref.shape[1] // GATE_PIECE
    n_blocks = SCAN_CHUNK // GATE_STEPS
    assert n_blocks % n_pieces == 0
    logits = [gate_logits(tb) for tb in range(GATE_LOOKAHEAD)]
    for tb in range(n_blocks):
        if tb + GATE_LOOKAHEAD < n_blocks:
            logits.append(gate_logits(tb + GATE_LOOKAHEAD))
        gates(tb, *logits.pop(0))
        if (tb + 1) % (n_blocks // n_pieces) == 0:
            gate_piece(tb // (n_blocks // n_pieces))

    def step_rows(d, g, i):
        t = i if d == 0 else SCAN_CHUNK - 1 - i
        return pl.ds(pl.multiple_of(t * n_chunks, n_chunks) + g * SUBLANES, SUBLANES)

    def chunk_rows(d, g, i):
        t = i if d == 0 else SCAN_CHUNK - 1 - i
        return pl.ds(g * SUBLANES * SCAN_PITCH + t, SUBLANES, stride=SCAN_PITCH)

    lanes = [(d, g) for d in range(2) for g in range(n_groups)]

    def totals(i, carry):
        out = []
        for (d, g), (prod, resp) in zip(lanes, carry):
            a = directions[d][0][step_rows(d, g, i), :]
            u = directions[d][1][step_rows(d, g, i), :]
            out.append((prod * a, a * resp + u))
        return tuple(out)

    init = tuple((jnp.ones((SUBLANES, hd), _F32), jnp.zeros((SUBLANES, hd), _F32)) for _ in lanes)
    tot = lax.fori_loop(0, SCAN_CHUNK, totals, init, unroll=SCAN_UNROLL)
    for (d, g), (prod, resp) in zip(lanes, tot):
        tot_ref[2 * d, pl.ds(g * SUBLANES, SUBLANES), :] = prod
        tot_ref[2 * d + 1, pl.ds(g * SUBLANES, SUBLANES), :] = resp

    for d in range(2):
        state = jnp.zeros((1, hd), _F32)
        for c in (range(n_chunks) if d == 0 else range(n_chunks - 1, -1, -1)):
            cin_ref[d, pl.ds(c, 1), :] = state
            state = tot_ref[2 * d, pl.ds(c, 1), :] * state + tot_ref[2 * d + 1, pl.ds(c, 1), :]

    def replay(i, carry):
        out = []
        for (d, g), h_in in zip(lanes, carry):
            a = directions[d][0][step_rows(d, g, i), :]
            u = directions[d][1][step_rows(d, g, i), :]
            h = a * h_in + u
            directions[d][2][chunk_rows(d, g, i), :] = h
            out.append(h)
        return tuple(out)

    lax.fori_loop(0, SCAN_CHUNK, replay,
                  tuple(cin_ref[d, pl.ds(g * SUBLANES, SUBLANES), :] for d, g in lanes), unroll=SCAN_UNROLL)

    def finish(c, carry):
        r0 = pl.multiple_of(c * SCAN_CHUNK, SCAN_CHUNK)
        p0 = pl.multiple_of(c * SCAN_PITCH, SUBLANES)
        h = hf_ref[pl.ds(p0, SCAN_CHUNK), :] + hb_ref[pl.ds(p0, SCAN_CHUNK), :]
        zg = zg_ref[0, pl.ds(r0, SCAN_CHUNK), :].astype(_F32)
        o_ref[0, pl.ds(r0, SCAN_CHUNK), :] = (h * _gelu_tanh(zg)).astype(o_ref.dtype)
        return carry

    lax.fori_loop(0, n_chunks, finish, 0)


def _lru_call(z_rec, z_gelu, conv_w, conv_b, wcat, bcat, lam, tokens, w_in, b_gates):
    b, s, w = z_rec.shape
    heads = w // HEAD_DIM
    n_chunks = s // SCAN_CHUNK
    assert s % (SCAN_CHUNK * SUBLANES) == 0
    t, d = tokens.shape
    tile = t // (b * heads)
    n_wrefs = 4
    wcols = GATE_WIDTH // n_wrefs
    first = (IN_WIDTH - GATE_WIDTH) // wcols
    assert (IN_WIDTH - GATE_WIDTH) % wcols == 0 and t % (b * heads) == 0
    blk = pl.BlockSpec((1, s, HEAD_DIM), lambda i, h: (i, 0, h))
    pitched = pltpu.VMEM((n_chunks * SCAN_PITCH, HEAD_DIM), _F32)
    time_major = pltpu.VMEM((s, HEAD_DIM), _F32)
    gate_w_specs = [
        pl.BlockSpec((d, wcols), lambda i, h, c=first + k: (0, c), pipeline_mode=pl.Buffered(1))
        for k in range(n_wrefs)]
    return pl.pallas_call(
        _lru_kernel,
        grid=(b, heads),
        in_specs=[
            blk,
            blk,
            pl.BlockSpec((conv_w.shape[0], HEAD_DIM), lambda i, h: (0, h)),
            pl.BlockSpec((1, HEAD_DIM), lambda i, h: (0, h)),
            pl.BlockSpec((1, HEAD_DIM, 4 * HEAD_DIM), lambda i, h: (h, 0, 0)),
            pl.BlockSpec((1, 1, 4 * HEAD_DIM), lambda i, h: (h, 0, 0)),
            pl.BlockSpec((1, 2, HEAD_DIM), lambda i, h: (h, 0, 0)),
            pl.BlockSpec((tile, d), lambda i, h: (i * heads + h, 0)),
            *gate_w_specs,
            pl.BlockSpec((1, GATE_WIDTH), lambda i, h: (0, 0)),
        ],
        out_specs=(blk, pl.BlockSpec((tile, GATE_WIDTH), lambda i, h: (i * heads + h, 0))),
        out_shape=(jax.ShapeDtypeStruct((b, s, w), _BF16), jax.ShapeDtypeStruct((t, GATE_WIDTH), _BF16)),
        scratch_shapes=[
            pltpu.VMEM((n_chunks * CONV_PITCH, HEAD_DIM), _F32),
            time_major, time_major, time_major, time_major, pitched, pitched,
            pltpu.VMEM((4, n_chunks, HEAD_DIM), _F32),
            pltpu.VMEM((2, n_chunks, HEAD_DIM), _F32),
        ],
        compiler_params=pltpu.CompilerParams(
            dimension_semantics=("parallel", "parallel"), vmem_limit_bytes=VMEM_LIMIT_BYTES),
        name="rglru_gates",
    )(z_rec, z_gelu, conv_w, conv_b, wcat, bcat, lam, tokens, w_in, w_in, w_in, w_in, b_gates)


def _merge_kernel(x_ref, yf_ref, yg_ref, gates_ref, pa_ref, pb_ref, wo_ref, *refs):
    n_casts = (len(refs) - 1) // 2
    cast_in, o_ref, cast_out = refs[:n_casts], refs[n_casts], refs[n_casts + 1:]
    d = x_ref.shape[1]
    y_a = _dot(yf_ref[...], pa_ref[...])
    y_b = _dot(yg_ref[...], pb_ref[...])
    merged = gates_ref[:, :d].astype(_F32) * y_a + gates_ref[:, d:].astype(_F32) * y_b
    o_ref[...] = x_ref[...] + _dot(merged.astype(_BF16), wo_ref[...])
    _cast_blocks(cast_in, cast_out)


def _merge_call(x, y_four, y_rec, gates, proj_a, proj_b, w_out, *, casts=(), tm=256):
    t, d = x.shape
    grid = (t // tm,)
    resident = functools.partial(pl.BlockSpec, index_map=lambda i: (0, 0), pipeline_mode=pl.Buffered(1))
    row_spec = pl.BlockSpec((tm, d), lambda i: (i, 0))
    cast_specs, cast_shapes = _ride_along_casts(casts, grid)
    return pl.pallas_call(
        _merge_kernel,
        grid=grid,
        in_specs=[
            row_spec,
            pl.BlockSpec((tm, y_four.shape[1]), lambda i: (i, 0)),
            pl.BlockSpec((tm, y_rec.shape[1]), lambda i: (i, 0)),
            pl.BlockSpec((tm, gates.shape[1]), lambda i: (i, 0)),
            resident(proj_a.shape),
            resident(proj_b.shape),
            resident(w_out.shape),
            *cast_specs,
        ],
        out_specs=(row_spec, *cast_specs),
        out_shape=(jax.ShapeDtypeStruct((t, d), _F32), *cast_shapes),
        compiler_params=pltpu.CompilerParams(
            dimension_semantics=("arbitrary",), vmem_limit_bytes=VMEM_LIMIT_BYTES),
        name="merge",
    )(x, y_four, y_rec, gates, proj_a, proj_b, w_out, *casts)


def kernel(x, ffn1_norm, ffn1_w_gate, ffn1_w_up, ffn1_w_down, mix_norm, w_in, b_gates, conv_w, conv_b, lru_wa, lru_ba, lru_wx, lru_bx, lru_lambda, proj_a, proj_b, w_out, ffn2_norm, ffn2_w_gate, ffn2_w_up, ffn2_w_down, final_norm):
    b, s, d = x.shape
    assert ffn1_norm.shape[0] == 1, "single-layer problem"
    bf = lambda w: w.astype(_BF16)
    x1, u, w_in_b = _ffn_call(
        x.reshape(b * s, d), ffn1_norm, bf(ffn1_w_gate[0]), bf(ffn1_w_up[0]), bf(ffn1_w_down[0]),
        mix_norm, emit_residual=True, casts=(w_in[0],), tm=512)
    z_four, z_rec, z_gelu, wg2, proj_a_b, proj_b_b, w_out_b = _inproj_call(
        u, w_in_b, casts=(ffn2_w_gate[0], proj_a[0], proj_b[0], w_out[0]))

    y_four, wd2 = _fourier_call(z_four.reshape(b, s, FOURIER_WIDTH), casts=(ffn2_w_down[0],))

    wcat = jnp.concatenate([lru_wa[0, 0], lru_wx[0, 0], lru_wa[0, 1], lru_wx[0, 1]], axis=-1)
    bcat = 0.5 * jnp.concatenate([lru_ba[0, 0], lru_bx[0, 0], lru_ba[0, 1], lru_bx[0, 1]], axis=-1)
    lam = jnp.transpose(lru_lambda[0], (1, 0, 2))
    y_rec, gates = _lru_call(z_rec.reshape(b, s, LRU_WIDTH), z_gelu.reshape(b, s, LRU_WIDTH),
                             conv_w[0], conv_b, bf(wcat), bcat[:, None, :], lam,
                             u, w_in_b, b_gates.reshape(1, GATE_WIDTH))

    x2, wu2 = _merge_call(x1, y_four.reshape(b * s, FOURIER_WIDTH), y_rec.reshape(b * s, LRU_WIDTH), gates,
                          proj_a_b, proj_b_b, w_out_b, casts=(ffn2_w_up[0],))
    (out,) = _ffn_call(x2, ffn2_norm, wg2, wu2, wd2, final_norm[None], emit_residual=False, tm=512)
    return out.reshape(b, s, d)
```
